```python
import math
import jax, jax.numpy as jnp
from jax import lax
import numpy as np

D_MODEL = 1024
BATCH = 4
SEQ = 4096
DEPTH = 4
DEC_BATCH = 16
DEC_SEQ = 2048
PAST_LEN = 128

D_A = D_MODEL // 2
CONV_A_WIDTH = 31
SSD_HEADS = 16
SSD_HEAD_DIM = 64
D_SSD = SSD_HEADS * SSD_HEAD_DIM
SSD_GROUPS = 2
SSD_STATE = 128
SSD_CONV = 5
CHUNK = 128
XBC_DIM = D_SSD + 2 * SSD_GROUPS * SSD_STATE
FOURIER_GROUPS = 4
FOURIER_GROUP_DIM = 128
D_C = FOURIER_GROUPS * FOURIER_GROUP_DIM
N_BRANCH = 3
D_FF = -(-8 * D_MODEL // (3 * 256)) * 256
IN_SIZES = (2 * D_A, D_SSD, XBC_DIM, 2 * SSD_HEADS, D_C, N_BRANCH * D_MODEL)
IN_COLS = sum(IN_SIZES)
EPS = 1e-6

kernel_name = "hybrid_conv_ssd_fnet_encoder"


def _split(t, sizes):
    offs = []
    s = 0
    for n in sizes[:-1]:
        s += n
        offs.append(s)
    return jnp.split(t, offs, axis=-1)


def _rmsnorm(x, g, out_dtype=None):
    x32 = x.astype(jnp.float32)
    y = x32 * lax.rsqrt(jnp.mean(x32 * x32, axis=-1, keepdims=True) + EPS) * g.astype(jnp.float32)
    return y.astype(out_dtype if out_dtype is not None else x.dtype)


def _layernorm(x, g, b):
    x32 = x.astype(jnp.float32)
    mu = jnp.mean(x32, axis=-1, keepdims=True)
    xc = x32 - mu
    y = xc * lax.rsqrt(jnp.mean(xc * xc, axis=-1, keepdims=True) + EPS)
    return (y * g.astype(jnp.float32) + b.astype(jnp.float32)).astype(x.dtype)


def _depthwise_conv(u, w, b):
    k = w.shape[0]
    out = lax.conv_general_dilated(
        u, w[:, None, :].astype(u.dtype), window_strides=(1,),
        padding=[(k // 2, k // 2)], dimension_numbers=("NWC", "WIO", "NWC"),
        feature_group_count=u.shape[-1])
    return out + b.astype(u.dtype)


def _ssd_chunked(xh, dt, a_coef, bm, cm):
    b, l, _, _ = xh.shape
    nc = l // CHUNK
    kk = SSD_HEADS // SSD_GROUPS
    f32 = jnp.float32
    x = (xh.astype(f32) * dt[..., None]).reshape(b, nc, CHUNK, SSD_GROUPS, kk, SSD_HEAD_DIM)
    a = (dt * a_coef.astype(f32)).reshape(b, nc, CHUNK, SSD_GROUPS, kk)
    a = jnp.transpose(a, (0, 1, 3, 4, 2))
    bc = bm.astype(f32).reshape(b, nc, CHUNK, SSD_GROUPS, SSD_STATE)
    cc = cm.astype(f32).reshape(b, nc, CHUNK, SSD_GROUPS, SSD_STATE)
    a_cs = jnp.cumsum(a, axis=-1)
    seg = a_cs[..., :, None] - a_cs[..., None, :]
    causal = jnp.tril(jnp.ones((CHUNK, CHUNK), dtype=bool))
    lmat = jnp.exp(jnp.where(causal, seg, -jnp.inf))
    scores = jnp.einsum("bclgn,bcsgn->bcgls", cc, bc)
    y_diag = jnp.einsum("bcgls,bcgkls,bcsgkp->bclgkp", scores, lmat, x)
    decay_to_end = jnp.exp(a_cs[..., -1:] - a_cs)
    states = jnp.einsum("bcsgn,bcgks,bcsgkp->bcgkpn", bc, decay_to_end, x)
    chunk_decay = jnp.exp(a_cs[..., -1])

    def step(h, inp):
        s, d = inp
        return h * d[..., None, None] + s, h

    h0 = jnp.zeros((b, SSD_GROUPS, kk, SSD_HEAD_DIM, SSD_STATE), f32)
    _, h_in = lax.scan(step, h0, (jnp.moveaxis(states, 1, 0), jnp.moveaxis(chunk_decay, 1, 0)))
    h_in = jnp.moveaxis(h_in, 0, 1)
    y_off = jnp.einsum("bclgn,bcgkpn,bcgkl->bclgkp", cc, h_in, jnp.exp(a_cs))
    return (y_diag + y_off).reshape(b, l, SSD_HEADS, SSD_HEAD_DIM)


def _fourier_mix(u):
    b, l, _ = u.shape
    ug = u.astype(jnp.float32).reshape(b, l, FOURIER_GROUPS, FOURIER_GROUP_DIM)
    f = jnp.fft.fft2(ug, axes=(1, 3), norm="ortho").real
    return f.reshape(b, l, D_C).astype(u.dtype)


def _mixer(h, w_in, conv_a_w, conv_a_b, ln_a_g, ln_a_b, w_a_out,
           conv_s_w, conv_s_b, dt_bias_f, dt_bias_b, a_log_f, a_log_b, d_skip, g_ssd, w_b_out,
           w_c_out, w_out):
    b, l, _ = h.shape
    proj = h @ w_in
    a_in, z, xbc, dt_raw, u_c, gate_raw = _split(proj, IN_SIZES)

    a_val, a_gate = _split(a_in, (D_A, D_A))
    a = a_val * jax.nn.sigmoid(a_gate)
    a = _depthwise_conv(a, conv_a_w, conv_a_b)
    a = jax.nn.silu(_layernorm(a, ln_a_g, ln_a_b))
    o_a = a @ w_a_out

    xbc = jax.nn.silu(_depthwise_conv(xbc, conv_s_w, conv_s_b))
    xs, bm, cm = _split(xbc, (D_SSD, SSD_GROUPS * SSD_STATE, SSD_GROUPS * SSD_STATE))
    xh = xs.reshape(b, l, SSD_HEADS, SSD_HEAD_DIM)
    bm = bm.reshape(b, l, SSD_GROUPS, SSD_STATE)
    cm = cm.reshape(b, l, SSD_GROUPS, SSD_STATE)
    dt_f_raw, dt_b_raw = _split(dt_raw.astype(jnp.float32), (SSD_HEADS, SSD_HEADS))
    dt_f = jax.nn.softplus(dt_f_raw + dt_bias_f.astype(jnp.float32))
    dt_b = jax.nn.softplus(dt_b_raw + dt_bias_b.astype(jnp.float32))
    a_f = -jnp.exp(a_log_f.astype(jnp.float32))
    a_b = -jnp.exp(a_log_b.astype(jnp.float32))
    y_f = _ssd_chunked(xh, dt_f, a_f, bm, cm)
    flip = lambda t: jnp.flip(t, axis=1)
    y_b = flip(_ssd_chunked(flip(xh), flip(dt_b), a_b, flip(bm), flip(cm)))
    y = y_f + y_b + d_skip.astype(jnp.float32)[:, None] * xh.astype(jnp.float32)
    y = y.reshape(b, l, D_SSD) * jax.nn.silu(z.astype(jnp.float32))
    y = _rmsnorm(y, g_ssd, out_dtype=h.dtype)
    o_b = y @ w_b_out

    o_c = _fourier_mix(u_c) @ w_c_out

    g_a, g_b, g_c = _split(jax.nn.sigmoid(gate_raw), (D_MODEL, D_MODEL, D_MODEL))
    merged = g_a * o_a + g_b * o_b + g_c * o_c
    return merged @ w_out


def _layer(x, c, w_ada, b_ada, g_pre_mix, g_post_mix, g_pre_ffn, g_post_ffn,
           w_in, conv_a_w, conv_a_b, ln_a_g, ln_a_b, w_a_out,
           conv_s_w, conv_s_b, dt_bias_f, dt_bias_b, a_log_f, a_log_b, d_skip, g_ssd, w_b_out,
           w_c_out, w_out, w_ffn_in, w_ffn_out):
    mod = jax.nn.silu(c) @ w_ada + b_ada
    sh1, sc1, gt1, sh2, sc2, gt2 = [m[:, None, :] for m in _split(mod, (D_MODEL,) * 6)]
    h = _rmsnorm(x, g_pre_mix) * (1 + sc1) + sh1
    m = _mixer(h, w_in, conv_a_w, conv_a_b, ln_a_g, ln_a_b, w_a_out,
               conv_s_w, conv_s_b, dt_bias_f, dt_bias_b, a_log_f, a_log_b, d_skip, g_ssd, w_b_out,
               w_c_out, w_out)
    x = x + gt1 * _rmsnorm(m, g_post_mix)
    h = _rmsnorm(x, g_pre_ffn) * (1 + sc2) + sh2
    gu = h @ w_ffn_in
    f = (jax.nn.silu(gu[..., :D_FF]) * gu[..., D_FF:]) @ w_ffn_out
    x = x + gt2 * _rmsnorm(f, g_post_ffn)
    return x


def setup_inputs(seed: int = 0) -> dict:
    key = jax.random.key(seed)
    ks = jax.random.split(key, 32)
    f32 = jnp.float32
    nrm = lambda k, shape, scale: jax.random.normal(k, shape, f32) * scale
    gain = lambda k, shape: 1.0 + 0.02 * jax.random.normal(k, shape, f32)
    dt0 = jnp.exp(jax.random.uniform(ks[16], (2, DEPTH, SSD_HEADS), f32, math.log(1e-3), math.log(1e-1)))
    dt_bias = dt0 + jnp.log(-jnp.expm1(-dt0))
    a_log = jnp.log(jax.random.uniform(ks[17], (2, DEPTH, SSD_HEADS), f32, 1.0, 16.0))
    return {
        "x_prompt": jax.random.normal(ks[0], (BATCH, SEQ, D_MODEL), f32),
        "x_sample": jax.random.normal(ks[1], (DEC_BATCH, DEC_SEQ, D_MODEL), f32),
        "c_prompt": jax.random.normal(ks[2], (BATCH, D_MODEL), f32),
        "c_sample": jax.random.normal(ks[3], (DEC_BATCH, D_MODEL), f32),
        "w_ada": nrm(ks[4], (DEPTH, D_MODEL, 6 * D_MODEL), 0.5 * D_MODEL ** -0.5),
        "b_ada": nrm(ks[5], (DEPTH, 6 * D_MODEL), 0.02),
        "g_pre_mix": gain(ks[6], (DEPTH, D_MODEL)),
        "g_post_mix": gain(ks[7], (DEPTH, D_MODEL)),
        "g_pre_ffn": gain(ks[8], (DEPTH, D_MODEL)),
        "g_post_ffn": gain(ks[9], (DEPTH, D_MODEL)),
        "w_in": nrm(ks[10], (DEPTH, D_MODEL, IN_COLS), D_MODEL ** -0.5),
        "conv_a_w": nrm(ks[11], (DEPTH, CONV_A_WIDTH, D_A), CONV_A_WIDTH ** -0.5),
        "conv_a_b": nrm(ks[12], (DEPTH, D_A), 0.02),
        "ln_a_g": gain(ks[13], (DEPTH, D_A)),
        "ln_a_b": nrm(ks[14], (DEPTH, D_A), 0.02),
        "w_a_out": nrm(ks[15], (DEPTH, D_A, D_MODEL), D_A ** -0.5),
        "conv_s_w": nrm(ks[18], (DEPTH, SSD_CONV, XBC_DIM), SSD_CONV ** -0.5),
        "conv_s_b": nrm(ks[19], (DEPTH, XBC_DIM), 0.02),
        "dt_bias_f": dt_bias[0],
        "dt_bias_b": dt_bias[1],
        "a_log_f": a_log[0],
        "a_log_b": a_log[1],
        "d_skip": gain(ks[20], (DEPTH, SSD_HEADS)),
        "g_ssd": gain(ks[21], (DEPTH, D_SSD)),
        "w_b_out": nrm(ks[22], (DEPTH, D_SSD, D_MODEL), D_SSD ** -0.5),
        "w_c_out": nrm(ks[23], (DEPTH, D_C, D_MODEL), D_C ** -0.5),
        "w_out": nrm(ks[24], (DEPTH, D_MODEL, D_MODEL), D_MODEL ** -0.5),
        "w_ffn_in": nrm(ks[25], (DEPTH, D_MODEL, 2 * D_FF), D_MODEL ** -0.5),
        "w_ffn_out": nrm(ks[26], (DEPTH, D_FF, D_MODEL), D_FF ** -0.5),
    }


def reference(x_prompt, x_sample, c_prompt, c_sample, w_ada, b_ada, g_pre_mix, g_post_mix,
              g_pre_ffn, g_post_ffn, w_in, conv_a_w, conv_a_b, ln_a_g, ln_a_b, w_a_out,
              conv_s_w, conv_s_b, dt_bias_f, dt_bias_b, a_log_f, a_log_b, d_skip, g_ssd, w_b_out,
              w_c_out, w_out, w_ffn_in, w_ffn_out):
    y_prompt = x_prompt
    y_sample = x_sample
    for i in range(DEPTH):
        lp = (w_ada[i], b_ada[i], g_pre_mix[i], g_post_mix[i], g_pre_ffn[i], g_post_ffn[i],
              w_in[i], conv_a_w[i], conv_a_b[i], ln_a_g[i], ln_a_b[i], w_a_out[i],
              conv_s_w[i], conv_s_b[i], dt_bias_f[i], dt_bias_b[i], a_log_f[i], a_log_b[i],
              d_skip[i], g_ssd[i], w_b_out[i], w_c_out[i], w_out[i], w_ffn_in[i], w_ffn_out[i])
        y_prompt = _layer(y_prompt, c_prompt, *lp)
        y_sample = _layer(y_sample, c_sample, *lp)
    return (y_prompt, y_sample)
```

```python
import functools
import math

import jax
import jax.numpy as jnp
from jax import lax
from jax.experimental import pallas as pl
from jax.experimental.pallas import tpu as pltpu

f32 = jnp.float32
bf16 = jnp.bfloat16
HIGHEST = lax.Precision.HIGHEST

D_MODEL = 1024
DEPTH = 4
D_A = 512
CONV_A_WIDTH = 31
SSD_HEADS = 16
SSD_HEAD_DIM = 64
D_SSD = SSD_HEADS * SSD_HEAD_DIM
SSD_GROUPS = 2
HEADS_PER_GROUP = SSD_HEADS // SSD_GROUPS
SSD_STATE = 128
SSD_CONV = 5
CHUNK = 128
BC_DIM = SSD_GROUPS * SSD_STATE
XBC_DIM = D_SSD + 2 * BC_DIM
D_C = 512
FOURIER_GROUPS = 4
FOURIER_GROUP_DIM = 128
D_FF = 2816
EPS = 1e-6

_OFF_A, _OFF_Z, _OFF_XBC, _OFF_DT, _OFF_UC, _OFF_GATE, _OFF_END = 0, 1024, 2048, 3584, 3616, 4128, 7200
_R_A, _R_Z, _R_XBC, _R_UC, _R_GATE, _R_END = 0, 1024, 2048, 3584, 4096, 7168

LANES = 128
SUBLANES = 8
VMEM_LIMIT_BYTES = 56 * 1024 * 1024

ROW_TILE = 512
CONV_A_HALO = 16
CONV_S_HALO = 8


def _sigmoid(x):
    return 1.0 / (1.0 + jnp.exp(-x))


def _softplus(x):
    return jnp.maximum(x, 0.0) + jnp.log1p(jnp.exp(-jnp.abs(x)))


def _params(semantics):
    return pltpu.CompilerParams(dimension_semantics=semantics, vmem_limit_bytes=VMEM_LIMIT_BYTES)


def _const_spec(shape):
    nd = len(shape)
    return pl.BlockSpec(shape, lambda *_: (0,) * nd, pipeline_mode=pl.Buffered(1))


def _mod_kernel(c_ref, w_ref, b_ref, o_ref):
    c = c_ref[...]
    sc = c * _sigmoid(c)
    o_ref[...] = jnp.dot(sc, w_ref[...], preferred_element_type=f32, precision=HIGHEST) + b_ref[...]


def _modulation(c_all, w_ada, b_ada):
    rows = c_all.shape[0]
    tn = 1024
    return pl.pallas_call(
        _mod_kernel,
        out_shape=jax.ShapeDtypeStruct((DEPTH, rows, 6 * D_MODEL), f32),
        grid=(DEPTH, 6 * D_MODEL // tn),
        in_specs=[
            pl.BlockSpec((rows, D_MODEL), lambda l, j: (0, 0)),
            pl.BlockSpec((None, D_MODEL, tn), lambda l, j: (l, 0, j)),
            pl.BlockSpec((None, 1, tn), lambda l, j: (l, 0, j)),
        ],
        out_specs=pl.BlockSpec((None, rows, tn), lambda l, j: (l, 0, j)),
        compiler_params=_params(("parallel", "parallel")),
        name="adaln_modulation",
    )(c_all, w_ada, b_ada.reshape(DEPTH, 1, 6 * D_MODEL))


def _in_proj_kernel(x_ref, mod_ref, g_ref, w_ref, wdt_ref, wdtt_ref, dtb_ref, dtbt_ref, dft_ref,
                    a_ref, sz_ref, xbc_ref, r_ref, gate_ref, dt_ref, dtt_ref):
    x = x_ref[...]
    ms = jnp.mean(x * x, axis=-1, keepdims=True)
    hn = x * lax.rsqrt(ms + EPS) * g_ref[...]
    h = (hn * (1.0 + mod_ref[1:2, :]) + mod_ref[0:1, :]).astype(bf16)

    def proj(c0, c1):
        return jnp.dot(h, w_ref[:, c0:c1], preferred_element_type=f32)

    a_in = proj(_R_A, _R_Z)
    a_ref[...] = a_in[:, :D_A] * _sigmoid(a_in[:, D_A:])
    z = proj(_R_Z, _R_XBC)
    sz_ref[...] = (z * _sigmoid(z)).astype(bf16)
    xbc_ref[...] = proj(_R_XBC, _R_UC)
    uc = proj(_R_UC, _R_GATE).astype(bf16)
    for g in range(FOURIER_GROUPS):
        lo, hi = g * FOURIER_GROUP_DIM, (g + 1) * FOURIER_GROUP_DIM
        cs = jnp.dot(uc[:, lo:hi], dft_ref[...], preferred_element_type=f32)
        r_ref[0, :, lo:hi] = cs[:, :FOURIER_GROUP_DIM].astype(bf16)
        r_ref[1, :, lo:hi] = cs[:, FOURIER_GROUP_DIM:].astype(bf16)
    for j in range(3):
        graw = proj(_R_GATE + j * D_MODEL, _R_GATE + (j + 1) * D_MODEL)
        gate_ref[:, j * D_MODEL:(j + 1) * D_MODEL] = _sigmoid(graw).astype(bf16)
    dt_ref[...] = _softplus(jnp.dot(h, wdt_ref[...], preferred_element_type=f32) + dtb_ref[...])
    dtt = lax.dot_general(wdtt_ref[...], h, (((1,), (1,)), ((), ())), preferred_element_type=f32)
    dtt_ref[...] = _softplus(dtt + dtbt_ref[...])


def _in_proj(x, mod, g_pre, w_r, w_dt, w_dtt, dtb, dtbt, dft_c, nb, seq):
    rows = nb * seq
    tm = min(ROW_TILE, seq)
    tiles_per_seq = seq // tm
    row_spec = lambda width: pl.BlockSpec((tm, width), lambda i: (i, 0))
    return pl.pallas_call(
        _in_proj_kernel,
        out_shape=(
            jax.ShapeDtypeStruct((rows, D_A), f32),
            jax.ShapeDtypeStruct((rows, D_SSD), bf16),
            jax.ShapeDtypeStruct((rows, XBC_DIM), f32),
            jax.ShapeDtypeStruct((2, seq, nb * D_C), bf16),
            jax.ShapeDtypeStruct((rows, 3 * D_MODEL), bf16),
            jax.ShapeDtypeStruct((rows, LANES), f32),
            jax.ShapeDtypeStruct((2 * SSD_HEADS, rows), f32),
        ),
        grid=(rows // tm,),
        in_specs=[
            row_spec(D_MODEL),
            pl.BlockSpec((None, 6, D_MODEL), lambda i: (i // tiles_per_seq, 0, 0)),
            _const_spec((1, D_MODEL)),
            _const_spec((D_MODEL, _R_END)),
            _const_spec((D_MODEL, LANES)),
            _const_spec((2 * SSD_HEADS, D_MODEL)),
            _const_spec((1, LANES)),
            _const_spec((2 * SSD_HEADS, 1)),
            _const_spec((FOURIER_GROUP_DIM, 2 * FOURIER_GROUP_DIM)),
        ],
        out_specs=(
            row_spec(D_A),
            row_spec(D_SSD),
            row_spec(XBC_DIM),
            pl.BlockSpec((2, tm, D_C), lambda i: (0, i % tiles_per_seq, i // tiles_per_seq)),
            row_spec(3 * D_MODEL),
            row_spec(LANES),
            pl.BlockSpec((2 * SSD_HEADS, tm), lambda i: (0, i)),
        ),
        compiler_params=_params(("parallel",)),
        name="in_proj",
    )(x, mod, g_pre, w_r, w_dt, w_dtt, dtb, dtbt, dft_c)


def _fill_padded(pad_ref, prev_ref, main_ref, next_ref, halo, tm, first, last):
    width = main_ref.shape[-1]
    pad_ref[pl.ds(halo, tm), :] = main_ref[...].astype(f32)
    zeros = jnp.zeros((halo, width), f32)
    pad_ref[pl.ds(0, halo), :] = jnp.where(first, zeros, prev_ref[...].astype(f32))
    pad_ref[pl.ds(halo + tm, halo), :] = jnp.where(last, zeros, next_ref[...].astype(f32))


def _depthwise(pad_ref, w_ref, base, taps, rows):
    acc = w_ref[0:1, :] * pad_ref[pl.ds(base, rows), :]
    for k in range(1, taps):
        acc = acc + w_ref[k:k + 1, :] * pad_ref[pl.ds(base + k, rows), :]
    return acc


def _conv_a_kernel(prev_ref, main_ref, next_ref, w_ref, b_ref, lg_ref, lb_ref, o_ref, pad_ref, *, tm, tiles_per_seq):
    i = pl.program_id(0)
    first = (i % tiles_per_seq) == 0
    last = (i % tiles_per_seq) == tiles_per_seq - 1
    _fill_padded(pad_ref, prev_ref, main_ref, next_ref, CONV_A_HALO, tm, first, last)
    rc = 128
    for r0 in range(0, tm, rc):
        acc = _depthwise(pad_ref, w_ref, r0 + CONV_A_HALO - CONV_A_WIDTH // 2, CONV_A_WIDTH, rc) + b_ref[...]
        mu = jnp.mean(acc, axis=-1, keepdims=True)
        xc = acc - mu
        y = xc * lax.rsqrt(jnp.mean(xc * xc, axis=-1, keepdims=True) + EPS)
        y = y * lg_ref[...] + lb_ref[...]
        o_ref[pl.ds(r0, rc), :] = (y * _sigmoid(y)).astype(bf16)


def _halo_specs(tm, halo, width, n_rows):
    per = tm // halo
    n_blocks = n_rows // halo
    return [
        pl.BlockSpec((halo, width), lambda i: (jnp.maximum(i * per - 1, 0), 0)),
        pl.BlockSpec((tm, width), lambda i: (i, 0)),
        pl.BlockSpec((halo, width), lambda i: (jnp.minimum((i + 1) * per, n_blocks - 1), 0)),
    ]


def _conv_a(a, w, b, lg, lb, nb, seq):
    rows = nb * seq
    tm = min(ROW_TILE, seq)
    kern = functools.partial(_conv_a_kernel, tm=tm, tiles_per_seq=seq // tm)
    return pl.pallas_call(
        kern,
        out_shape=jax.ShapeDtypeStruct((rows, D_A), bf16),
        grid=(rows // tm,),
        in_specs=_halo_specs(tm, CONV_A_HALO, D_A, rows) + [
            _const_spec((CONV_A_WIDTH, D_A)), _const_spec((1, D_A)), _const_spec((1, D_A)), _const_spec((1, D_A))],
        out_specs=pl.BlockSpec((tm, D_A), lambda i: (i, 0)),
        scratch_shapes=[pltpu.VMEM((tm + 2 * CONV_A_HALO, D_A), f32)],
        compiler_params=_params(("parallel",)),
        name="conv_branch",
    )(a, a, a, w, b, lg, lb)


def _conv_s_kernel(prev_ref, main_ref, next_ref, w_ref, b_ref, o_ref, pad_ref, *, tm, tiles_per_seq):
    i = pl.program_id(0)
    first = (i % tiles_per_seq) == 0
    last = (i % tiles_per_seq) == tiles_per_seq - 1
    _fill_padded(pad_ref, prev_ref, main_ref, next_ref, CONV_S_HALO, tm, first, last)
    rc = 128
    for r0 in range(0, tm, rc):
        acc = _depthwise(pad_ref, w_ref, r0 + CONV_S_HALO - SSD_CONV // 2, SSD_CONV, rc) + b_ref[...]
        o_ref[pl.ds(r0, rc), :] = (acc * _sigmoid(acc)).astype(bf16)


def _conv_s(xbc, w, b, nb, seq):
    rows = nb * seq
    tm = min(ROW_TILE, seq)
    kern = functools.partial(_conv_s_kernel, tm=tm, tiles_per_seq=seq // tm)
    return pl.pallas_call(
        kern,
        out_shape=jax.ShapeDtypeStruct((rows, XBC_DIM), bf16),
        grid=(rows // tm,),
        in_specs=_halo_specs(tm, CONV_S_HALO, XBC_DIM, rows) + [_const_spec((SSD_CONV, XBC_DIM)), _const_spec((1, XBC_DIM))],
        out_specs=pl.BlockSpec((tm, XBC_DIM), lambda i: (i, 0)),
        scratch_shapes=[pltpu.VMEM((tm + 2 * CONV_S_HALO, XBC_DIM), f32)],
        compiler_params=_params(("parallel",)),
        name="ssd_short_conv",
    )(xbc, xbc, xbc, w, b)


def _ssd_direction(xbc_ref, dt, dtt, a_row, a_col, expand_ref, state_ref, reverse):
    q = CHUNK
    row_i = lax.broadcasted_iota(jnp.int32, (q, q), 0)
    col_i = lax.broadcasted_iota(jnp.int32, (q, q), 1)
    if reverse:
        keep = row_i <= col_i
        cum_left = (col_i >= row_i).astype(f32)
        cum_right = (row_i >= col_i).astype(f32)
        edge = 0
    else:
        keep = row_i >= col_i
        cum_left = (col_i <= row_i).astype(f32)
        cum_right = (row_i <= col_i).astype(f32)
        edge = q - 1
    a = dt * a_row
    at = dtt * a_col
    cs = jnp.dot(cum_left, a, preferred_element_type=f32, precision=HIGHEST)
    cst = jnp.dot(at, cum_right, preferred_element_type=f32, precision=HIGHEST)
    cs_edge = cs[edge:edge + 1, :]
    expand = expand_ref[...]

    def lanes(v):
        return jnp.dot(v, expand, preferred_element_type=f32, precision=HIGHEST)

    w_in = lanes(jnp.exp(cs_edge - cs) * dt)
    e_out = lanes(jnp.exp(cs))
    e_chunk = lanes(jnp.exp(cs_edge))

    xs = xbc_ref[:, 0:D_SSD]
    gw = HEADS_PER_GROUP * SSD_HEAD_DIM
    pieces = []
    for g in range(SSD_GROUPS):
        bm = xbc_ref[:, D_SSD + g * SSD_STATE:D_SSD + (g + 1) * SSD_STATE]
        cm = xbc_ref[:, D_SSD + BC_DIM + g * SSD_STATE:D_SSD + BC_DIM + (g + 1) * SSD_STATE]
        scores = lax.dot_general(cm, bm, (((1,), (1,)), ((), ())), preferred_element_type=f32)
        xg = xs[:, g * gw:(g + 1) * gw]
        h_in = state_ref[g]
        y_off = jnp.dot(cm, h_in.astype(bf16), preferred_element_type=f32) * e_out[:, g * gw:(g + 1) * gw]
        xw = (xg.astype(f32) * w_in[:, g * gw:(g + 1) * gw]).astype(bf16)
        states = lax.dot_general(bm, xw, (((0,), (0,)), ((), ())), preferred_element_type=f32)
        state_ref[g] = h_in * e_chunk[:, g * gw:(g + 1) * gw] + states
        for k in range(HEADS_PER_GROUP):
            hd = g * HEADS_PER_GROUP + k
            seg = cs[:, hd:hd + 1] - cst[hd:hd + 1, :]
            m = scores * jnp.where(keep, jnp.exp(seg), 0.0) * dtt[hd:hd + 1, :]
            xh = xg[:, k * SSD_HEAD_DIM:(k + 1) * SSD_HEAD_DIM]
            y_diag = jnp.dot(m.astype(bf16), xh, preferred_element_type=f32)
            pieces.append(y_diag + y_off[:, k * SSD_HEAD_DIM:(k + 1) * SSD_HEAD_DIM])
    return jnp.concatenate(pieces, axis=-1)


def _ssd_fwd_kernel(xbc_ref, dt_ref, dtt_ref, arow_ref, acol_ref, expand_ref, y_ref, state_ref):
    @pl.when(pl.program_id(1) == 0)
    def _():
        state_ref[...] = jnp.zeros_like(state_ref)

    y_ref[...] = _ssd_direction(xbc_ref, dt_ref[:, 0:SSD_HEADS], dtt_ref[0:SSD_HEADS, :],
                                arow_ref[...], acol_ref[...], expand_ref, state_ref, reverse=False)


def _ssd_bwd_kernel(xbc_ref, dt_ref, dtt_ref, arow_ref, acol_ref, expand_ref, yf_ref, sz_ref, dskip_ref, g_ref,
                    y_ref, state_ref):
    @pl.when(pl.program_id(1) == 0)
    def _():
        state_ref[...] = jnp.zeros_like(state_ref)

    y_b = _ssd_direction(xbc_ref, dt_ref[:, SSD_HEADS:2 * SSD_HEADS], dtt_ref[SSD_HEADS:2 * SSD_HEADS, :],
                         arow_ref[...], acol_ref[...], expand_ref, state_ref, reverse=True)
    y = yf_ref[...] + y_b + dskip_ref[...] * xbc_ref[:, 0:D_SSD].astype(f32)
    y = y * sz_ref[...].astype(f32)
    y = y * lax.rsqrt(jnp.mean(y * y, axis=-1, keepdims=True) + EPS) * g_ref[...]
    y_ref[...] = y.astype(bf16)


def _ssd(xbc_c, dt, dtt, sz, a_f, a_b, expand, dskip_lanes, g_ssd, nb, seq):
    rows = nb * seq
    nc = seq // CHUNK
    state = pltpu.VMEM((SSD_GROUPS, SSD_STATE, HEADS_PER_GROUP * SSD_HEAD_DIM), f32)

    def chunk_specs(pos):
        return [
            pl.BlockSpec((CHUNK, XBC_DIM), lambda b, c: (b * nc + pos(c), 0)),
            pl.BlockSpec((CHUNK, LANES), lambda b, c: (b * nc + pos(c), 0)),
            pl.BlockSpec((2 * SSD_HEADS, CHUNK), lambda b, c: (0, b * nc + pos(c))),
            _const_spec((1, SSD_HEADS)), _const_spec((SSD_HEADS, 1)), _const_spec((SSD_HEADS, D_SSD)),
        ]

    fwd = lambda c: c
    y_f = pl.pallas_call(
        _ssd_fwd_kernel,
        out_shape=jax.ShapeDtypeStruct((rows, D_SSD), f32),
        grid=(nb, nc),
        in_specs=chunk_specs(fwd),
        out_specs=pl.BlockSpec((CHUNK, D_SSD), lambda b, c: (b * nc + c, 0)),
        scratch_shapes=[state],
        compiler_params=_params(("parallel", "arbitrary")),
        name="ssd_forward_scan",
    )(xbc_c, dt, dtt, a_f.reshape(1, SSD_HEADS), a_f.reshape(SSD_HEADS, 1), expand)
    rev = lambda c: nc - 1 - c
    row_spec = lambda width: pl.BlockSpec((CHUNK, width), lambda b, c: (b * nc + rev(c), 0))
    return pl.pallas_call(
        _ssd_bwd_kernel,
        out_shape=jax.ShapeDtypeStruct((rows, D_SSD), bf16),
        grid=(nb, nc),
        in_specs=chunk_specs(rev) + [row_spec(D_SSD), row_spec(D_SSD), _const_spec((1, D_SSD)), _const_spec((1, D_SSD))],
        out_specs=row_spec(D_SSD),
        scratch_shapes=[state],
        compiler_params=_params(("parallel", "arbitrary")),
        name="ssd_backward_scan",
    )(xbc_c, dt, dtt, a_b.reshape(1, SSD_HEADS), a_b.reshape(SSD_HEADS, 1), expand, y_f, sz, dskip_lanes, g_ssd)


def _seq_dft_kernel(a_ref, r_ref, o_ref, acc_ref, *, scale):
    k = pl.program_id(2)

    @pl.when(k == 0)
    def _():
        acc_ref[...] = jnp.zeros_like(acc_ref)

    acc_ref[...] += jnp.dot(a_ref[...], r_ref[...], preferred_element_type=f32)

    @pl.when(k == pl.num_programs(2) - 1)
    def _():
        o_ref[...] = acc_ref[...] * scale


def _seq_dft(dft_l, r, seq):
    cols = r.shape[1]
    tm = min(1024, seq)
    tn = min(1024, cols)
    tk = min(2048, 2 * seq)
    scale = 1.0 / math.sqrt(seq * FOURIER_GROUP_DIM)
    return pl.pallas_call(
        functools.partial(_seq_dft_kernel, scale=scale),
        out_shape=jax.ShapeDtypeStruct((seq, cols), f32),
        grid=(seq // tm, cols // tn, 2 * seq // tk),
        in_specs=[pl.BlockSpec((tm, tk), lambda i, j, k: (i, k)), pl.BlockSpec((tk, tn), lambda i, j, k: (k, j))],
        out_specs=pl.BlockSpec((tm, tn), lambda i, j, k: (i, j)),
        scratch_shapes=[pltpu.VMEM((tm, tn), f32)],
        compiler_params=_params(("parallel", "parallel", "arbitrary")),
        name="fourier_seq_dft",
    )(dft_l, r)


def _merge_kernel(x_ref, mod_ref, a_ref, yb_ref, fc_ref, gate_ref, wa_ref, wb_ref, wc_ref, wo_ref, g_ref, o_ref):
    o_a = jnp.dot(a_ref[...], wa_ref[...], preferred_element_type=f32)
    merged = gate_ref[:, 0:D_MODEL].astype(f32) * o_a
    o_b = jnp.dot(yb_ref[...], wb_ref[...], preferred_element_type=f32)
    merged = merged + gate_ref[:, D_MODEL:2 * D_MODEL].astype(f32) * o_b
    o_c = jnp.dot(fc_ref[...].astype(bf16), wc_ref[...], preferred_element_type=f32)
    merged = merged + gate_ref[:, 2 * D_MODEL:3 * D_MODEL].astype(f32) * o_c
    m = jnp.dot(merged.astype(bf16), wo_ref[...], preferred_element_type=f32)
    mn = m * lax.rsqrt(jnp.mean(m * m, axis=-1, keepdims=True) + EPS) * g_ref[...]
    o_ref[...] = x_ref[...] + mod_ref[2:3, :] * mn


def _merge(x, mod, a2, yb, fc, gates, wa, wb, wc, wo, g_post, nb, seq):
    rows = nb * seq
    tm = min(ROW_TILE, seq)
    tiles_per_seq = seq // tm
    row_spec = lambda width: pl.BlockSpec((tm, width), lambda i: (i, 0))
    return pl.pallas_call(
        _merge_kernel,
        out_shape=jax.ShapeDtypeStruct((rows, D_MODEL), f32),
        grid=(rows // tm,),
        in_specs=[
            row_spec(D_MODEL),
            pl.BlockSpec((None, 6, D_MODEL), lambda i: (i // tiles_per_seq, 0, 0)),
            row_spec(D_A), row_spec(D_SSD),
            pl.BlockSpec((tm, D_C), lambda i: (i % tiles_per_seq, i // tiles_per_seq)),
            row_spec(3 * D_MODEL),
            _const_spec((D_A, D_MODEL)), _const_spec((D_SSD, D_MODEL)), _const_spec((D_C, D_MODEL)),
            _const_spec((D_MODEL, D_MODEL)), _const_spec((1, D_MODEL)),
        ],
        out_specs=row_spec(D_MODEL),
        compiler_params=_params(("parallel",)),
        name="merge_out_proj",
    )(x, mod, a2, yb, fc, gates, wa, wb, wc, wo, g_post)


FFN_CHUNK = D_FF // 2


def _ffn_kernel(x_ref, mod_ref, gpre_ref, wi_ref, wo_ref, gpost_ref, o_ref):
    x = x_ref[...]
    hn = x * lax.rsqrt(jnp.mean(x * x, axis=-1, keepdims=True) + EPS) * gpre_ref[...]
    h = (hn * (1.0 + mod_ref[4:5, :]) + mod_ref[3:4, :]).astype(bf16)
    f = None
    for c in range(D_FF // FFN_CHUNK):
        lo = c * FFN_CHUNK
        gate = jnp.dot(h, wi_ref[:, lo:lo + FFN_CHUNK], preferred_element_type=f32)
        up = jnp.dot(h, wi_ref[:, D_FF + lo:D_FF + lo + FFN_CHUNK], preferred_element_type=f32)
        act = (gate * _sigmoid(gate) * up).astype(bf16)
        part = jnp.dot(act, wo_ref[lo:lo + FFN_CHUNK, :], preferred_element_type=f32)
        f = part if f is None else f + part
    fn = f * lax.rsqrt(jnp.mean(f * f, axis=-1, keepdims=True) + EPS) * gpost_ref[...]
    o_ref[...] = x + mod_ref[5:6, :] * fn


def _ffn(x, mod, g_pre, wi, wo, g_post, nb, seq):
    rows = nb * seq
    tm = min(ROW_TILE, seq)
    tiles_per_seq = seq // tm
    row_spec = pl.BlockSpec((tm, D_MODEL), lambda i: (i, 0))
    return pl.pallas_call(
        _ffn_kernel,
        out_shape=jax.ShapeDtypeStruct((rows, D_MODEL), f32),
        grid=(rows // tm,),
        in_specs=[
            row_spec,
            pl.BlockSpec((None, 6, D_MODEL), lambda i: (i // tiles_per_seq, 0, 0)),
            _const_spec((1, D_MODEL)), _const_spec((D_MODEL, 2 * D_FF)), _const_spec((D_FF, D_MODEL)),
            _const_spec((1, D_MODEL)),
        ],
        out_specs=row_spec,
        compiler_params=_params(("parallel",)),
        name="ffn",
    )(x, mod, g_pre, wi, wo, g_post)


def _dft_cos_sin(n):
    j = lax.broadcasted_iota(jnp.int32, (n, n), 0)
    k = lax.broadcasted_iota(jnp.int32, (n, n), 1)
    ang = ((j * k) % n).astype(f32) * (2.0 * math.pi / n)
    return jnp.cos(ang), jnp.sin(ang)


def _layer(x, mod, lw, consts, nb, seq):
    a, sz, xbc, r, gates, dt, dtt = _in_proj(
        x, mod, lw["g_pre_mix"], lw["w_r"], lw["w_dt"], lw["w_dtt"], lw["dtb"], lw["dtbt"], consts["dft_c"], nb, seq)
    a2 = _conv_a(a, lw["conv_a_w"], lw["conv_a_b"], lw["ln_a_g"], lw["ln_a_b"], nb, seq)
    xbc_c = _conv_s(xbc, lw["conv_s_w"], lw["conv_s_b"], nb, seq)
    yb = _ssd(xbc_c, dt, dtt, sz, lw["a_f"], lw["a_b"], consts["expand"], lw["dskip"], lw["g_ssd"], nb, seq)
    fc = _seq_dft(consts["dft_l"][seq], r.reshape(2 * seq, nb * D_C), seq)
    x1 = _merge(x, mod, a2, yb, fc, gates, lw["w_a_out"], lw["w_b_out"], lw["w_c_out"], lw["w_out"],
                lw["g_post_mix"], nb, seq)
    return _ffn(x1, mod, lw["g_pre_ffn"], lw["w_ffn_in"], lw["w_ffn_out"], lw["g_post_ffn"], nb, seq)


def kernel(x_prompt, x_sample, c_prompt, c_sample, w_ada, b_ada, g_pre_mix, g_post_mix, g_pre_ffn, g_post_ffn, w_in, conv_a_w, conv_a_b, ln_a_g, ln_a_b, w_a_out, conv_s_w, conv_s_b, dt_bias_f, dt_bias_b, a_log_f, a_log_b, d_skip, g_ssd, w_b_out, w_c_out, w_out, w_ffn_in, w_ffn_out):
    depth = w_in.shape[0]
    assert depth == DEPTH
    bp, lp, _ = x_prompt.shape
    bs, ls, _ = x_sample.shape

    n_seq = bp + bs
    pad_rows = -n_seq % SUBLANES
    c_all = jnp.concatenate([c_prompt, c_sample, jnp.zeros((pad_rows, D_MODEL), f32)], axis=0)
    mod_all = _modulation(c_all, w_ada, b_ada).reshape(depth, n_seq + pad_rows, 6, D_MODEL)

    w_r = jnp.concatenate([w_in[:, :, _OFF_A:_OFF_DT], w_in[:, :, _OFF_UC:_OFF_END]], axis=-1).astype(bf16)
    w_dt_raw = w_in[:, :, _OFF_DT:_OFF_UC]
    w_dt = jnp.pad(w_dt_raw, ((0, 0), (0, 0), (0, LANES - 2 * SSD_HEADS))).astype(bf16)
    w_dtt = jnp.swapaxes(w_dt_raw, 1, 2).astype(bf16)
    dtb_cat = jnp.concatenate([dt_bias_f, dt_bias_b], axis=-1)
    dtb = jnp.pad(dtb_cat, ((0, 0), (0, LANES - 2 * SSD_HEADS)))
    a_f = -jnp.exp(a_log_f)
    a_b = -jnp.exp(a_log_b)

    cos_c, sin_c = _dft_cos_sin(FOURIER_GROUP_DIM)
    consts = {
        "dft_c": jnp.concatenate([cos_c, sin_c], axis=1).astype(bf16),
        "expand": (lax.broadcasted_iota(jnp.int32, (SSD_HEADS, D_SSD), 1) // SSD_HEAD_DIM
                   == lax.broadcasted_iota(jnp.int32, (SSD_HEADS, D_SSD), 0)).astype(f32),
        "dft_l": {},
    }
    for seq in sorted({lp, ls}):
        cos_l, sin_l = _dft_cos_sin(seq)
        consts["dft_l"][seq] = jnp.concatenate([cos_l, -sin_l], axis=1).astype(bf16)

    row = lambda v: v.reshape(1, -1)
    y_prompt = x_prompt.reshape(bp * lp, D_MODEL)
    y_sample = x_sample.reshape(bs * ls, D_MODEL)
    for i in range(depth):
        lw = {
            "g_pre_mix": row(g_pre_mix[i]), "g_post_mix": row(g_post_mix[i]),
            "g_pre_ffn": row(g_pre_ffn[i]), "g_post_ffn": row(g_post_ffn[i]),
            "w_r": w_r[i], "w_dt": w_dt[i], "w_dtt": w_dtt[i],
            "dtb": row(dtb[i]), "dtbt": dtb_cat[i].reshape(-1, 1),
            "conv_a_w": conv_a_w[i], "conv_a_b": row(conv_a_b[i]), "ln_a_g": row(ln_a_g[i]), "ln_a_b": row(ln_a_b[i]),
            "conv_s_w": conv_s_w[i], "conv_s_b": row(conv_s_b[i]),
            "a_f": a_f[i], "a_b": a_b[i],
            "dskip": row(jnp.repeat(d_skip[i], SSD_HEAD_DIM)), "g_ssd": row(g_ssd[i]),
            "w_a_out": w_a_out[i].astype(bf16), "w_b_out": w_b_out[i].astype(bf16),
            "w_c_out": w_c_out[i].astype(bf16), "w_out": w_out[i].astype(bf16),
            "w_ffn_in": w_ffn_in[i].astype(bf16), "w_ffn_out": w_ffn_out[i].astype(bf16),
        }
        y_prompt = _layer(y_prompt, mod_all[i, :bp], lw, consts, bp, lp)
        y_sample = _layer(y_sample, mod_all[i, bp:bp + bs], lw, consts, bs, ls)
    return (y_prompt.reshape(bp, lp, D_MODEL), y_sample.reshape(bs, ls, D_MODEL))
```

```python
import functools
import math

import jax
import jax.numpy as jnp
from jax import lax
from jax.experimental import pallas as pl
from jax.experimental.pallas import tpu as pltpu

f32 = jnp.float32
bf16 = jnp.bfloat16
HIGHEST = lax.Precision.HIGHEST

D_MODEL = 1024
DEPTH = 4
D_A = 512
CONV_A_WIDTH = 31
SSD_HEADS = 16
SSD_HEAD_DIM = 64
D_SSD = SSD_HEADS * SSD_HEAD_DIM
SSD_GROUPS = 2
HEADS_PER_GROUP = SSD_HEADS // SSD_GROUPS
SSD_STATE = 128
SSD_CONV = 5
CHUNK = 128
BC_DIM = SSD_GROUPS * SSD_STATE
XBC_DIM = D_SSD + 2 * BC_DIM
D_C = 512
FOURIER_GROUPS = 4
FOURIER_GROUP_DIM = 128
D_FF = 2816
EPS = 1e-6

_OFF_A, _OFF_Z, _OFF_XBC, _OFF_DT, _OFF_UC, _OFF_GATE, _OFF_END = 0, 1024, 2048, 3584, 3616, 4128, 7200
_R_A, _R_Z, _R_XBC, _R_UC, _R_GATE, _R_END = 0, 1024, 2048, 3584, 4096, 7168

LANES = 128
SUBLANES = 8
VMEM_LIMIT_BYTES = 56 * 1024 * 1024

ROW_TILE = 512
CONV_A_HALO = 16
CONV_S_HALO = 8


def _sigmoid(x):
    return 1.0 / (1.0 + jnp.exp(-x))


def _softplus(x):
    return jnp.maximum(x, 0.0) + jnp.log1p(jnp.exp(-jnp.abs(x)))


def _params(semantics):
    return pltpu.CompilerParams(dimension_semantics=semantics, vmem_limit_bytes=VMEM_LIMIT_BYTES)


def _const_spec(shape):
    nd = len(shape)
    return pl.BlockSpec(shape, lambda *_: (0,) * nd, pipeline_mode=pl.Buffered(1))


def _mod_kernel(c_ref, w_ref, b_ref, o_ref):
    c = c_ref[...]
    sc = c * _sigmoid(c)
    o_ref[...] = jnp.dot(sc, w_ref[...], preferred_element_type=f32, precision=HIGHEST) + b_ref[...]


def _modulation(c_all, w_ada, b_ada):
    rows = c_all.shape[0]
    tn = 1024
    return pl.pallas_call(
        _mod_kernel,
        out_shape=jax.ShapeDtypeStruct((DEPTH, rows, 6 * D_MODEL), f32),
        grid=(DEPTH, 6 * D_MODEL // tn),
        in_specs=[
            pl.BlockSpec((rows, D_MODEL), lambda l, j: (0, 0)),
            pl.BlockSpec((None, D_MODEL, tn), lambda l, j: (l, 0, j)),
            pl.BlockSpec((None, 1, tn), lambda l, j: (l, 0, j)),
        ],
        out_specs=pl.BlockSpec((None, rows, tn), lambda l, j: (l, 0, j)),
        compiler_params=_params(("parallel", "parallel")),
        name="adaln_modulation",
    )(c_all, w_ada, b_ada.reshape(DEPTH, 1, 6 * D_MODEL))


def _in_proj_kernel(x_ref, mod_ref, g_ref, w_ref, wdt_ref, wdtt_ref, dtb_ref, dtbt_ref, dft_ref,
                    a_ref, sz_ref, xbc_ref, r_ref, gate_ref, dt_ref, dtt_ref):
    x = x_ref[...]
    ms = jnp.mean(x * x, axis=-1, keepdims=True)
    hn = x * lax.rsqrt(ms + EPS) * g_ref[...]
    h = (hn * (1.0 + mod_ref[1:2, :]) + mod_ref[0:1, :]).astype(bf16)

    def proj(c0, c1):
        return jnp.dot(h, w_ref[:, c0:c1], preferred_element_type=f32)

    a_in = proj(_R_A, _R_Z)
    a_ref[...] = a_in[:, :D_A] * _sigmoid(a_in[:, D_A:])
    z = proj(_R_Z, _R_XBC)
    sz_ref[...] = (z * _sigmoid(z)).astype(bf16)
    xbc_ref[...] = proj(_R_XBC, _R_UC)
    uc = proj(_R_UC, _R_GATE).astype(bf16)
    for g in range(FOURIER_GROUPS):
        lo, hi = g * FOURIER_GROUP_DIM, (g + 1) * FOURIER_GROUP_DIM
        cs = jnp.dot(uc[:, lo:hi], dft_ref[...], preferred_element_type=f32)
        r_ref[0, :, lo:hi] = cs[:, :FOURIER_GROUP_DIM].astype(bf16)
        r_ref[1, :, lo:hi] = cs[:, FOURIER_GROUP_DIM:].astype(bf16)
    for j in range(3):
        graw = proj(_R_GATE + j * D_MODEL, _R_GATE + (j + 1) * D_MODEL)
        gate_ref[:, j * D_MODEL:(j + 1) * D_MODEL] = _sigmoid(graw).astype(bf16)
    dt_ref[...] = _softplus(jnp.dot(h, wdt_ref[...], preferred_element_type=f32) + dtb_ref[...])
    dtt = lax.dot_general(wdtt_ref[...], h, (((1,), (1,)), ((), ())), preferred_element_type=f32)
    dtt_ref[...] = _softplus(dtt + dtbt_ref[...])


def _in_proj(x, mod, g_pre, w_r, w_dt, w_dtt, dtb, dtbt, dft_c, nb, seq):
    rows = nb * seq
    tm = min(ROW_TILE, seq)
    tiles_per_seq = seq // tm
    row_spec = lambda width: pl.BlockSpec((tm, width), lambda i: (i, 0))
    return pl.pallas_call(
        _in_proj_kernel,
        out_shape=(
            jax.ShapeDtypeStruct((rows, D_A), f32),
            jax.ShapeDtypeStruct((rows, D_SSD), bf16),
            jax.ShapeDtypeStruct((rows, XBC_DIM), f32),
            jax.ShapeDtypeStruct((2, seq, nb * D_C), bf16),
            jax.ShapeDtypeStruct((rows, 3 * D_MODEL), bf16),
            jax.ShapeDtypeStruct((rows, LANES), f32),
            jax.ShapeDtypeStruct((2 * SSD_HEADS, rows), f32),
        ),
        grid=(rows // tm,),
        in_specs=[
            row_spec(D_MODEL),
            pl.BlockSpec((None, 6, D_MODEL), lambda i: (i // tiles_per_seq, 0, 0)),
            _const_spec((1, D_MODEL)),
            _const_spec((D_MODEL, _R_END)),
            _const_spec((D_MODEL, LANES)),
            _const_spec((2 * SSD_HEADS, D_MODEL)),
            _const_spec((1, LANES)),
            _const_spec((2 * SSD_HEADS, 1)),
            _const_spec((FOURIER_GROUP_DIM, 2 * FOURIER_GROUP_DIM)),
        ],
        out_specs=(
            row_spec(D_A),
            row_spec(D_SSD),
            row_spec(XBC_DIM),
            pl.BlockSpec((2, tm, D_C), lambda i: (0, i % tiles_per_seq, i // tiles_per_seq)),
            row_spec(3 * D_MODEL),
            row_spec(LANES),
            pl.BlockSpec((2 * SSD_HEADS, tm), lambda i: (0, i)),
        ),
        compiler_params=_params(("parallel",)),
        name="in_proj",
    )(x, mod, g_pre, w_r, w_dt, w_dtt, dtb, dtbt, dft_c)


def _fill_padded(pad_ref, prev_ref, main_ref, next_ref, halo, tm, first, last):
    width = main_ref.shape[-1]
    pad_ref[pl.ds(halo, tm), :] = main_ref[...].astype(f32)
    zeros = jnp.zeros((halo, width), f32)
    pad_ref[pl.ds(0, halo), :] = jnp.where(first, zeros, prev_ref[...].astype(f32))
    pad_ref[pl.ds(halo + tm, halo), :] = jnp.where(last, zeros, next_ref[...].astype(f32))


def _depthwise(pad_ref, w_ref, base, taps, rows):
    acc = w_ref[0:1, :] * pad_ref[pl.ds(base, rows), :]
    for k in range(1, taps):
        acc = acc + w_ref[k:k + 1, :] * pad_ref[pl.ds(base + k, rows), :]
    return acc


def _depthwise_aligned(pad_ref, shift_ref, w_ref, base, taps, r0, rows):
    acc = None
    for k in range(taps):
        q, rho = divmod(base + k, SUBLANES)
        src = pad_ref if rho == 0 else shift_ref.at[rho - 1]
        term = w_ref[k:k + 1, :] * src[pl.ds(r0 + q * SUBLANES, rows), :]
        acc = term if acc is None else acc + term
    return acc


def _conv_a_kernel(prev_ref, main_ref, next_ref, w_ref, b_ref, lg_ref, lb_ref, o_ref, pad_ref, shift_ref, *,
                   tm, tiles_per_seq):
    i = pl.program_id(0)
    first = (i % tiles_per_seq) == 0
    last = (i % tiles_per_seq) == tiles_per_seq - 1
    _fill_padded(pad_ref, prev_ref, main_ref, next_ref, CONV_A_HALO, tm, first, last)
    span = tm + 2 * CONV_A_HALO - SUBLANES
    for rho in range(1, SUBLANES):
        shift_ref[rho - 1, pl.ds(0, span), :] = pad_ref[pl.ds(rho, span), :]
    rc = 128
    base = CONV_A_HALO - CONV_A_WIDTH // 2
    for r0 in range(0, tm, rc):
        acc = _depthwise_aligned(pad_ref, shift_ref, w_ref, base, CONV_A_WIDTH, r0, rc) + b_ref[...]
        mu = jnp.mean(acc, axis=-1, keepdims=True)
        xc = acc - mu
        y = xc * lax.rsqrt(jnp.mean(xc * xc, axis=-1, keepdims=True) + EPS)
        y = y * lg_ref[...] + lb_ref[...]
        o_ref[pl.ds(r0, rc), :] = (y * _sigmoid(y)).astype(bf16)


def _halo_specs(tm, halo, width, n_rows):
    per = tm // halo
    n_blocks = n_rows // halo
    return [
        pl.BlockSpec((halo, width), lambda i: (jnp.maximum(i * per - 1, 0), 0)),
        pl.BlockSpec((tm, width), lambda i: (i, 0)),
        pl.BlockSpec((halo, width), lambda i: (jnp.minimum((i + 1) * per, n_blocks - 1), 0)),
    ]


def _conv_a(a, w, b, lg, lb, nb, seq):
    rows = nb * seq
    tm = min(ROW_TILE, seq)
    kern = functools.partial(_conv_a_kernel, tm=tm, tiles_per_seq=seq // tm)
    return pl.pallas_call(
        kern,
        out_shape=jax.ShapeDtypeStruct((rows, D_A), bf16),
        grid=(rows // tm,),
        in_specs=_halo_specs(tm, CONV_A_HALO, D_A, rows) + [
            _const_spec((CONV_A_WIDTH, D_A)), _const_spec((1, D_A)), _const_spec((1, D_A)), _const_spec((1, D_A))],
        out_specs=pl.BlockSpec((tm, D_A), lambda i: (i, 0)),
        scratch_shapes=[pltpu.VMEM((tm + 2 * CONV_A_HALO, D_A), f32),
                        pltpu.VMEM((SUBLANES - 1, tm + 2 * CONV_A_HALO, D_A), f32)],
        compiler_params=_params(("parallel",)),
        name="conv_branch",
    )(a, a, a, w, b, lg, lb)


def _conv_s_kernel(prev_ref, main_ref, next_ref, w_ref, b_ref, o_ref, pad_ref, *, tm, tiles_per_seq):
    i = pl.program_id(0)
    first = (i % tiles_per_seq) == 0
    last = (i % tiles_per_seq) == tiles_per_seq - 1
    _fill_padded(pad_ref, prev_ref, main_ref, next_ref, CONV_S_HALO, tm, first, last)
    rc = 128
    for r0 in range(0, tm, rc):
        acc = _depthwise(pad_ref, w_ref, r0 + CONV_S_HALO - SSD_CONV // 2, SSD_CONV, rc) + b_ref[...]
        o_ref[pl.ds(r0, rc), :] = (acc * _sigmoid(acc)).astype(bf16)


def _conv_s(xbc, w, b, nb, seq):
    rows = nb * seq
    tm = min(ROW_TILE, seq)
    kern = functools.partial(_conv_s_kernel, tm=tm, tiles_per_seq=seq // tm)
    return pl.pallas_call(
        kern,
        out_shape=jax.ShapeDtypeStruct((rows, XBC_DIM), bf16),
        grid=(rows // tm,),
        in_specs=_halo_specs(tm, CONV_S_HALO, XBC_DIM, rows) + [_const_spec((SSD_CONV, XBC_DIM)), _const_spec((1, XBC_DIM))],
        out_specs=pl.BlockSpec((tm, XBC_DIM), lambda i: (i, 0)),
        scratch_shapes=[pltpu.VMEM((tm + 2 * CONV_S_HALO, XBC_DIM), f32)],
        compiler_params=_params(("parallel",)),
        name="ssd_short_conv",
    )(xbc, xbc, xbc, w, b)


HALF = CHUNK // 2
GROUP_W = HEADS_PER_GROUP * SSD_HEAD_DIM
N_PAIRS = SSD_HEADS // 2
PAIRS_PER_GROUP = N_PAIRS // SSD_GROUPS
REPLICA = 2 * SSD_HEADS


def _split3(v):
    hi = v.astype(bf16)
    r1 = v - hi.astype(f32)
    mid = r1.astype(bf16)
    lo = (r1 - mid.astype(f32)).astype(bf16)
    return hi, mid, lo


def _pack3(v):
    hi, mid, lo = _split3(v)
    lane = lax.broadcasted_iota(jnp.int32, v.shape, 1)
    zero = jnp.zeros_like(hi)
    return jnp.where(lane < REPLICA, hi, jnp.where(lane < 2 * REPLICA, mid, jnp.where(lane < 3 * REPLICA, lo, zero)))


def _chunk_cumsum_rows(a, tri_l3_ref):
    hi, mid, lo = _split3(a)
    a3 = jnp.concatenate([hi, mid, lo], axis=0)
    return jnp.dot(tri_l3_ref[...], a3, preferred_element_type=f32)


def _decay_compact(a, tri_l3_ref):
    cs = _chunk_cumsum_rows(a, tri_l3_ref)
    tot = cs[CHUNK - 1:CHUNK, :]
    lane = lax.broadcasted_iota(jnp.int32, a.shape, 1)
    is_bwd = (lane % REPLICA) >= SSD_HEADS
    return jnp.where(is_bwd, tot - cs + a, cs), tot


STATE_CHUNKS = 4


def _ssd_state_kernel(xf_ref, dtf_ref, xb_ref, dtb_ref, arow_ref, tri_l3_ref, ef_ref, eb_ref,
                      hf_ref, hb_ref, state_ref):
    @pl.when(pl.program_id(1) == 0)
    def _():
        state_ref[...] = jnp.zeros_like(state_ref)

    lane = lax.broadcasted_iota(jnp.int32, (CHUNK, LANES), 1)
    is_bwd = (lane % REPLICA) >= SSD_HEADS
    steps = []
    for i in range(STATE_CHUNKS):
        rf = pl.ds(i * CHUNK, CHUNK)
        rb = pl.ds((STATE_CHUNKS - 1 - i) * CHUNK, CHUNK)
        dt = jnp.where(is_bwd, dtb_ref[rb, :], dtf_ref[rf, :])
        c, tot = _decay_compact(dt * arow_ref[...], tri_l3_ref)
        w = _pack3(jnp.exp2(tot - c) * dt)
        e_chunk = _pack3(jnp.broadcast_to(jnp.exp2(tot), (SUBLANES, LANES)))
        per_dir = []
        for x_ref, rr, e_ref in ((xf_ref, rf, ef_ref), (xb_ref, rb, eb_ref)):
            w_x = jnp.dot(w, e_ref[...], preferred_element_type=f32)
            decay_x = jnp.dot(e_chunk, e_ref[...], preferred_element_type=f32)[0:1, :]
            states = []
            for g in range(SSD_GROUPS):
                lo = g * GROUP_W
                bm = x_ref[rr, D_SSD + g * SSD_STATE:D_SSD + (g + 1) * SSD_STATE]
                xw = (x_ref[rr, lo:lo + GROUP_W].astype(f32) * w_x[:, lo:lo + GROUP_W]).astype(bf16)
                states.append(lax.dot_general(bm, xw, (((0,), (0,)), ((), ())), preferred_element_type=f32))
            per_dir.append((decay_x, states))
        steps.append(per_dir)
    for i in range(STATE_CHUNKS):
        for d, h_ref, slot in ((0, hf_ref, i), (1, hb_ref, STATE_CHUNKS - 1 - i)):
            decay_x, states = steps[i][d]
            for g in range(SSD_GROUPS):
                lo = g * GROUP_W
                h_in = state_ref[d, g]
                h_ref[slot, g] = h_in.astype(bf16)
                state_ref[d, g] = h_in * decay_x[:, lo:lo + GROUP_W] + states[g]


def _ssd_chunk_kernel(x_ref, dt_ref, dtt_ref, hf_ref, hb_ref, sz_ref, arow_ref, acol_ref, tri_l3_ref, tri_u3_ref,
                      ef_ref, eb_ref, dskip_ref, g_ref, y_ref, rows_ref, acc_ref):
    q = CHUNK
    dt = dt_ref[...]
    dtt = dtt_ref[...]
    c, _ = _decay_compact(dt * arow_ref[...], tri_l3_ref)
    at = dtt * acol_ref[...]
    hi, mid, lo = _split3(at)
    cst = jnp.dot(jnp.concatenate([hi, mid, lo], axis=1), tri_u3_ref[...], preferred_element_type=f32)
    row = lax.broadcasted_iota(jnp.int32, at.shape, 0)
    ct = jnp.where(row >= SSD_HEADS, cst[:, q - 1:q] - cst + at, cst)
    ldt = jnp.log2(dtt)

    lane8 = lax.broadcasted_iota(jnp.int32, (N_PAIRS, LANES), 1)
    kind = 0
    for src in (ct - ldt, ldt):
        for d in range(2):
            even = src[d * SSD_HEADS:d * SSD_HEADS + N_PAIRS, :]
            odd = src[d * SSD_HEADS + N_PAIRS:(d + 1) * SSD_HEADS, :]
            rows_ref[pl.ds(kind * N_PAIRS, N_PAIRS), :] = jnp.where(lane8 < HALF, even, pltpu.roll(odd, HALF, 1))
            rows_ref[pl.ds((kind + 1) * N_PAIRS, N_PAIRS), :] = jnp.where(lane8 < HALF, pltpu.roll(even, HALF, 1), odd)
            kind += 2

    c3 = _pack3(c)
    cx = (jnp.dot(c3, ef_ref[...], preferred_element_type=f32),
          jnp.dot(c3, eb_ref[...], preferred_element_type=f32))

    l_in_half = lax.broadcasted_iota(jnp.int32, (HALF, LANES), 0)
    lane_h = lax.broadcasted_iota(jnp.int32, (HALF, LANES), 1)
    s_in_half = lane_h % HALF
    zero_h = jnp.zeros((HALF, LANES), bf16)
    for g in range(SSD_GROUPS):
        bm = x_ref[:, D_SSD + g * SSD_STATE:D_SSD + (g + 1) * SSD_STATE]
        cm = x_ref[:, D_SSD + BC_DIM + g * SSD_STATE:D_SSD + BC_DIM + (g + 1) * SSD_STATE]
        sc = []
        for half in range(2):
            b_half = bm[half * HALF:(half + 1) * HALF, :]
            b_dup = jnp.concatenate([b_half, b_half], axis=0)
            sc.append(lax.dot_general(cm, b_dup, (((1,), (1,)), ((), ())), preferred_element_type=f32))
        low = [jnp.where(l_in_half >= s_in_half, sc[h][h * HALF:(h + 1) * HALF, :], 0.0) for h in range(2)]
        up = [jnp.where(l_in_half <= s_in_half, sc[h][h * HALF:(h + 1) * HALF, :], 0.0) for h in range(2)]
        lo_g = g * GROUP_W
        y_off = (jnp.dot(cm, hf_ref[g], preferred_element_type=f32) * jnp.exp2(cx[0][:, lo_g:lo_g + GROUP_W])
                 + jnp.dot(cm, hb_ref[g], preferred_element_type=f32) * jnp.exp2(cx[1][:, lo_g:lo_g + GROUP_W]))
        acc_ref[:, lo_g:lo_g + GROUP_W] = y_off
        for jj in range(PAIRS_PER_GROUP):
            j = g * PAIRS_PER_GROUP + jj
            lo_j = j * LANES
            row_vec = lambda k: rows_ref[pl.ds(k * N_PAIRS + j, 1), :]
            quads = [[None, None], [None, None]]
            for th in range(2):
                rows_t = slice(th * HALF, (th + 1) * HALF)
                for sh in range(2):
                    if th == sh:
                        seg_f = jnp.minimum(cx[0][rows_t, lo_j:lo_j + LANES] - row_vec(0 + sh), row_vec(4 + sh))
                        seg_b = jnp.minimum(cx[1][rows_t, lo_j:lo_j + LANES] - row_vec(2 + sh), row_vec(6 + sh))
                        t = low[sh] * jnp.exp2(seg_f) + up[sh] * jnp.exp2(seg_b)
                    else:
                        d = 0 if th > sh else 1
                        seg = cx[d][rows_t, lo_j:lo_j + LANES] - row_vec(2 * d + sh)
                        t = sc[sh][rows_t, :] * jnp.exp2(seg)
                    quads[th][sh] = t.astype(bf16)
            lhs = jnp.concatenate([jnp.concatenate(quads[0], axis=1), jnp.concatenate(quads[1], axis=1)], axis=0)
            xp = x_ref[:, lo_j:lo_j + LANES]
            blocks = []
            for sh in range(2):
                xh = xp[sh * HALF:(sh + 1) * HALF, :]
                blocks.append(jnp.where(lane_h < HALF, xh, zero_h))
                blocks.append(jnp.where(lane_h >= HALF, xh, zero_h))
            rhs = jnp.concatenate(blocks, axis=0)
            acc_ref[:, lo_j:lo_j + LANES] += jnp.dot(lhs, rhs, preferred_element_type=f32)

    y = acc_ref[...] + dskip_ref[...] * x_ref[:, 0:D_SSD].astype(f32)
    y = y * sz_ref[...].astype(f32)
    y = y * lax.rsqrt(jnp.mean(y * y, axis=-1, keepdims=True) + EPS) * g_ref[...]
    y_ref[...] = y.astype(bf16)


def _ssd(xbc_c, dt, dtt, sz, a_row, a_col, consts, dskip_lanes, g_ssd, nb, seq):
    rows = nb * seq
    nc = seq // CHUNK
    state_shape = (SSD_GROUPS, SSD_STATE, GROUP_W)
    assert nc % STATE_CHUNKS == 0
    nblk = nc // STATE_CHUNKS
    blk_rows = STATE_CHUNKS * CHUNK
    x_spec = lambda pos: pl.BlockSpec((blk_rows, XBC_DIM), lambda b, k: (b * nblk + pos(k), 0))
    dt_spec = lambda pos: pl.BlockSpec((blk_rows, LANES), lambda b, k: (b * nblk + pos(k), 0))
    h_spec = lambda pos: pl.BlockSpec((STATE_CHUNKS,) + state_shape, lambda b, k: (b * nblk + pos(k), 0, 0, 0))
    fwd = lambda k: k
    rev = lambda k: nblk - 1 - k
    h_shape = jax.ShapeDtypeStruct((nb * nc,) + state_shape, bf16)
    h_f, h_b = pl.pallas_call(
        _ssd_state_kernel,
        out_shape=(h_shape, h_shape),
        grid=(nb, nblk),
        in_specs=[x_spec(fwd), dt_spec(fwd), x_spec(rev), dt_spec(rev),
                  _const_spec((1, LANES)), _const_spec((CHUNK, 3 * CHUNK)),
                  _const_spec((LANES, D_SSD)), _const_spec((LANES, D_SSD))],
        out_specs=(h_spec(fwd), h_spec(rev)),
        scratch_shapes=[pltpu.VMEM((2,) + state_shape, f32)],
        compiler_params=_params(("parallel", "arbitrary")),
        name="ssd_state_scan",
    )(xbc_c, dt, xbc_c, dt, a_row, consts["tri_l3"], consts["expand_f"], consts["expand_b"])

    row_spec = lambda width: pl.BlockSpec((CHUNK, width), lambda i: (i, 0))
    hc_spec = pl.BlockSpec((None,) + state_shape, lambda i: (i, 0, 0, 0))
    return pl.pallas_call(
        _ssd_chunk_kernel,
        out_shape=jax.ShapeDtypeStruct((rows, D_SSD), bf16),
        grid=(nb * nc,),
        in_specs=[row_spec(XBC_DIM), row_spec(LANES), pl.BlockSpec((2 * SSD_HEADS, CHUNK), lambda i: (0, i)),
                  hc_spec, hc_spec, row_spec(D_SSD),
                  _const_spec((1, LANES)), _const_spec((2 * SSD_HEADS, 1)),
                  _const_spec((CHUNK, 3 * CHUNK)), _const_spec((3 * CHUNK, CHUNK)),
                  _const_spec((LANES, D_SSD)), _const_spec((LANES, D_SSD)),
                  _const_spec((1, D_SSD)), _const_spec((1, D_SSD))],
        out_specs=row_spec(D_SSD),
        scratch_shapes=[pltpu.VMEM((8 * N_PAIRS, LANES), f32), pltpu.VMEM((CHUNK, D_SSD), f32)],
        compiler_params=_params(("parallel",)),
        name="ssd_chunk",
    )(xbc_c, dt, dtt, h_f, h_b, sz, a_row, a_col, consts["tri_l3"], consts["tri_u3"],
      consts["expand_f"], consts["expand_b"], dskip_lanes, g_ssd)


def _seq_dft_kernel(a_ref, r_ref, o_ref, acc_ref, *, scale):
    k = pl.program_id(2)

    @pl.when(k == 0)
    def _():
        acc_ref[...] = jnp.zeros_like(acc_ref)

    acc_ref[...] += jnp.dot(a_ref[...], r_ref[...], preferred_element_type=f32)

    @pl.when(k == pl.num_programs(2) - 1)
    def _():
        o_ref[...] = acc_ref[...] * scale


def _seq_dft(dft_l, r, seq):
    cols = r.shape[1]
    tm = min(1024, seq)
    tn = min(1024, cols)
    tk = min(2048, 2 * seq)
    scale = 1.0 / math.sqrt(seq * FOURIER_GROUP_DIM)
    return pl.pallas_call(
        functools.partial(_seq_dft_kernel, scale=scale),
        out_shape=jax.ShapeDtypeStruct((seq, cols), f32),
        grid=(seq // tm, cols // tn, 2 * seq // tk),
        in_specs=[pl.BlockSpec((tm, tk), lambda i, j, k: (i, k)), pl.BlockSpec((tk, tn), lambda i, j, k: (k, j))],
        out_specs=pl.BlockSpec((tm, tn), lambda i, j, k: (i, j)),
        scratch_shapes=[pltpu.VMEM((tm, tn), f32)],
        compiler_params=_params(("parallel", "parallel", "arbitrary")),
        name="fourier_seq_dft",
    )(dft_l, r)


def _merge_kernel(x_ref, mod_ref, a_ref, yb_ref, fc_ref, gate_ref, wa_ref, wb_ref, wc_ref, wo_ref, g_ref, o_ref):
    o_a = jnp.dot(a_ref[...], wa_ref[...], preferred_element_type=f32)
    merged = gate_ref[:, 0:D_MODEL].astype(f32) * o_a
    o_b = jnp.dot(yb_ref[...], wb_ref[...], preferred_element_type=f32)
    merged = merged + gate_ref[:, D_MODEL:2 * D_MODEL].astype(f32) * o_b
    o_c = jnp.dot(fc_ref[...].astype(bf16), wc_ref[...], preferred_element_type=f32)
    merged = merged + gate_ref[:, 2 * D_MODEL:3 * D_MODEL].astype(f32) * o_c
    m = jnp.dot(merged.astype(bf16), wo_ref[...], preferred_element_type=f32)
    mn = m * lax.rsqrt(jnp.mean(m * m, axis=-1, keepdims=True) + EPS) * g_ref[...]
    o_ref[...] = x_ref[...] + mod_ref[2:3, :] * mn


def _merge(x, mod, a2, yb, fc, gates, wa, wb, wc, wo, g_post, nb, seq):
    rows = nb * seq
    tm = min(ROW_TILE, seq)
    tiles_per_seq = seq // tm
    row_spec = lambda width: pl.BlockSpec((tm, width), lambda i: (i, 0))
    return pl.pallas_call(
        _merge_kernel,
        out_shape=jax.ShapeDtypeStruct((rows, D_MODEL), f32),
        grid=(rows // tm,),
        in_specs=[
            row_spec(D_MODEL),
            pl.BlockSpec((None, 6, D_MODEL), lambda i: (i // tiles_per_seq, 0, 0)),
            row_spec(D_A), row_spec(D_SSD),
            pl.BlockSpec((tm, D_C), lambda i: (i % tiles_per_seq, i // tiles_per_seq)),
            row_spec(3 * D_MODEL),
            _const_spec((D_A, D_MODEL)), _const_spec((D_SSD, D_MODEL)), _const_spec((D_C, D_MODEL)),
            _const_spec((D_MODEL, D_MODEL)), _const_spec((1, D_MODEL)),
        ],
        out_specs=row_spec(D_MODEL),
        compiler_params=_params(("parallel",)),
        name="merge_out_proj",
    )(x, mod, a2, yb, fc, gates, wa, wb, wc, wo, g_post)


FFN_CHUNK = D_FF // 2


def _ffn_kernel(x_ref, mod_ref, gpre_ref, wi_ref, wo_ref, gpost_ref, o_ref):
    x = x_ref[...]
    hn = x * lax.rsqrt(jnp.mean(x * x, axis=-1, keepdims=True) + EPS) * gpre_ref[...]
    h = (hn * (1.0 + mod_ref[4:5, :]) + mod_ref[3:4, :]).astype(bf16)
    f = None
    for c in range(D_FF // FFN_CHUNK):
        lo = c * FFN_CHUNK
        gate = jnp.dot(h, wi_ref[:, lo:lo + FFN_CHUNK], preferred_element_type=f32)
        up = jnp.dot(h, wi_ref[:, D_FF + lo:D_FF + lo + FFN_CHUNK], preferred_element_type=f32)
        act = (gate * _sigmoid(gate) * up).astype(bf16)
        part = jnp.dot(act, wo_ref[lo:lo + FFN_CHUNK, :], preferred_element_type=f32)
        f = part if f is None else f + part
    fn = f * lax.rsqrt(jnp.mean(f * f, axis=-1, keepdims=True) + EPS) * gpost_ref[...]
    o_ref[...] = x + mod_ref[5:6, :] * fn


def _ffn(x, mod, g_pre, wi, wo, g_post, nb, seq):
    rows = nb * seq
    tm = min(ROW_TILE, seq)
    tiles_per_seq = seq // tm
    row_spec = pl.BlockSpec((tm, D_MODEL), lambda i: (i, 0))
    return pl.pallas_call(
        _ffn_kernel,
        out_shape=jax.ShapeDtypeStruct((rows, D_MODEL), f32),
        grid=(rows // tm,),
        in_specs=[
            row_spec,
            pl.BlockSpec((None, 6, D_MODEL), lambda i: (i // tiles_per_seq, 0, 0)),
            _const_spec((1, D_MODEL)), _const_spec((D_MODEL, 2 * D_FF)), _const_spec((D_FF, D_MODEL)),
            _const_spec((1, D_MODEL)),
        ],
        out_specs=row_spec,
        compiler_params=_params(("parallel",)),
        name="ffn",
    )(x, mod, g_pre, wi, wo, g_post)


def _dft_cos_sin(n):
    j = lax.broadcasted_iota(jnp.int32, (n, n), 0)
    k = lax.broadcasted_iota(jnp.int32, (n, n), 1)
    ang = ((j * k) % n).astype(f32) * (2.0 * math.pi / n)
    return jnp.cos(ang), jnp.sin(ang)


def _layer(x, mod, lw, consts, nb, seq):
    a, sz, xbc, r, gates, dt, dtt = _in_proj(
        x, mod, lw["g_pre_mix"], lw["w_r"], lw["w_dt"], lw["w_dtt"], lw["dtb"], lw["dtbt"], consts["dft_c"], nb, seq)
    a2 = _conv_a(a, lw["conv_a_w"], lw["conv_a_b"], lw["ln_a_g"], lw["ln_a_b"], nb, seq)
    xbc_c = _conv_s(xbc, lw["conv_s_w"], lw["conv_s_b"], nb, seq)
    yb = _ssd(xbc_c, dt, dtt, sz, lw["a_row"], lw["a_col"], consts, lw["dskip"], lw["g_ssd"], nb, seq)
    fc = _seq_dft(consts["dft_l"][seq], r.reshape(2 * seq, nb * D_C), seq)
    x1 = _merge(x, mod, a2, yb, fc, gates, lw["w_a_out"], lw["w_b_out"], lw["w_c_out"], lw["w_out"],
                lw["g_post_mix"], nb, seq)
    return _ffn(x1, mod, lw["g_pre_ffn"], lw["w_ffn_in"], lw["w_ffn_out"], lw["g_post_ffn"], nb, seq)


def kernel(x_prompt, x_sample, c_prompt, c_sample, w_ada, b_ada, g_pre_mix, g_post_mix, g_pre_ffn, g_post_ffn, w_in, conv_a_w, conv_a_b, ln_a_g, ln_a_b, w_a_out, conv_s_w, conv_s_b, dt_bias_f, dt_bias_b, a_log_f, a_log_b, d_skip, g_ssd, w_b_out, w_c_out, w_out, w_ffn_in, w_ffn_out):
    depth = w_in.shape[0]
    assert depth == DEPTH
    bp, lp, _ = x_prompt.shape
    bs, ls, _ = x_sample.shape

    n_seq = bp + bs
    pad_rows = -n_seq % SUBLANES
    c_all = jnp.concatenate([c_prompt, c_sample, jnp.zeros((pad_rows, D_MODEL), f32)], axis=0)
    mod_all = _modulation(c_all, w_ada, b_ada).reshape(depth, n_seq + pad_rows, 6, D_MODEL)

    w_r = jnp.concatenate([w_in[:, :, _OFF_A:_OFF_DT], w_in[:, :, _OFF_UC:_OFF_END]], axis=-1).astype(bf16)
    reps = LANES // REPLICA
    t_order = jnp.array([d * SSD_HEADS + 2 * j + par for d in range(2) for par in range(2) for j in range(N_PAIRS)])
    w_dt_raw = w_in[:, :, _OFF_DT:_OFF_UC]
    w_dt = jnp.tile(w_dt_raw, (1, 1, reps)).astype(bf16)
    w_dtt = jnp.swapaxes(w_dt_raw[:, :, t_order], 1, 2).astype(bf16)
    dtb_cat = jnp.concatenate([dt_bias_f, dt_bias_b], axis=-1)
    dtb = jnp.tile(dtb_cat, (1, reps))
    a_cat = jnp.concatenate([-jnp.exp(a_log_f), -jnp.exp(a_log_b)], axis=-1) * math.log2(math.e)
    a_row = jnp.tile(a_cat, (1, reps))

    cos_c, sin_c = _dft_cos_sin(FOURIER_GROUP_DIM)
    t_i = lax.broadcasted_iota(jnp.int32, (CHUNK, CHUNK), 0)
    u_i = lax.broadcasted_iota(jnp.int32, (CHUNK, CHUNK), 1)
    tri = (u_i <= t_i).astype(bf16)
    src_lane = lax.broadcasted_iota(jnp.int32, (LANES, D_SSD), 0)
    dst_head = lax.broadcasted_iota(jnp.int32, (LANES, D_SSD), 1) // SSD_HEAD_DIM
    live = src_lane < 3 * REPLICA
    consts = {
        "dft_c": jnp.concatenate([cos_c, sin_c], axis=1).astype(bf16),
        "tri_l3": jnp.tile(tri, (1, 3)),
        "tri_u3": jnp.tile(tri.T, (3, 1)),
        "expand_f": (live & (src_lane % REPLICA == dst_head)).astype(bf16),
        "expand_b": (live & (src_lane % REPLICA == dst_head + SSD_HEADS)).astype(bf16),
        "dft_l": {},
    }
    for seq in sorted({lp, ls}):
        cos_l, sin_l = _dft_cos_sin(seq)
        consts["dft_l"][seq] = jnp.concatenate([cos_l, -sin_l], axis=1).astype(bf16)

    row = lambda v: v.reshape(1, -1)
    y_prompt = x_prompt.reshape(bp * lp, D_MODEL)
    y_sample = x_sample.reshape(bs * ls, D_MODEL)
    for i in range(depth):
        lw = {
            "g_pre_mix": row(g_pre_mix[i]), "g_post_mix": row(g_post_mix[i]),
            "g_pre_ffn": row(g_pre_ffn[i]), "g_post_ffn": row(g_post_ffn[i]),
            "w_r": w_r[i], "w_dt": w_dt[i], "w_dtt": w_dtt[i],
            "dtb": row(dtb[i]), "dtbt": dtb_cat[i][t_order].reshape(-1, 1),
            "conv_a_w": conv_a_w[i], "conv_a_b": row(conv_a_b[i]), "ln_a_g": row(ln_a_g[i]), "ln_a_b": row(ln_a_b[i]),
            "conv_s_w": conv_s_w[i], "conv_s_b": row(conv_s_b[i]),
            "a_row": row(a_row[i]), "a_col": a_cat[i][t_order].reshape(-1, 1),
            "dskip": row(jnp.repeat(d_skip[i], SSD_HEAD_DIM)), "g_ssd": row(g_ssd[i]),
            "w_a_out": w_a_out[i].astype(bf16), "w_b_out": w_b_out[i].astype(bf16),
            "w_c_out": w_c_out[i].astype(bf16), "w_out": w_out[i].astype(bf16),
            "w_ffn_in": w_ffn_in[i].astype(bf16), "w_ffn_out": w_ffn_out[i].astype(bf16),
        }
        y_prompt = _layer(y_prompt, mod_all[i, :bp], lw, consts, bp, lp)
        y_sample = _layer(y_sample, mod_all[i, bp:bp + bs], lw, consts, bs, ls)
    return (y_prompt.reshape(bp, lp, D_MODEL), y_sample.reshape(bs, ls, D_MODEL))
```

```python
import functools
import math

import jax
import jax.numpy as jnp
from jax import lax
from jax.experimental import pallas as pl
from jax.experimental.pallas import tpu as pltpu

f32 = jnp.float32
bf16 = jnp.bfloat16
HIGHEST = lax.Precision.HIGHEST

D_MODEL = 1024
DEPTH = 4
D_A = 512
CONV_A_WIDTH = 31
SSD_HEADS = 16
SSD_HEAD_DIM = 64
D_SSD = SSD_HEADS * SSD_HEAD_DIM
SSD_GROUPS = 2
HEADS_PER_GROUP = SSD_HEADS // SSD_GROUPS
SSD_STATE = 128
SSD_CONV = 5
CHUNK = 128
BC_DIM = SSD_GROUPS * SSD_STATE
XBC_DIM = D_SSD + 2 * BC_DIM
D_C = 512
FOURIER_GROUPS = 4
FOURIER_GROUP_DIM = 128
D_FF = 2816
EPS = 1e-6

_OFF_A, _OFF_Z, _OFF_XBC, _OFF_DT, _OFF_UC, _OFF_GATE, _OFF_END = 0, 1024, 2048, 3584, 3616, 4128, 7200
_R_A, _R_Z, _R_XBC, _R_UC, _R_GATE, _R_END = 0, 1024, 2048, 3584, 4096, 7168

LANES = 128
SUBLANES = 8
VMEM_LIMIT_BYTES = 56 * 1024 * 1024

ROW_TILE = 512
CONV_A_HALO = 16
CONV_S_HALO = 8
CONV_BLOCK_ROWS = 16


def _sigmoid(x):
    return 1.0 / (1.0 + jnp.exp(-x))


def _softplus(x):
    return jnp.maximum(x, 0.0) + jnp.log1p(jnp.exp(-jnp.abs(x)))


def _params(semantics):
    return pltpu.CompilerParams(dimension_semantics=semantics, vmem_limit_bytes=VMEM_LIMIT_BYTES)


def _const_spec(shape):
    nd = len(shape)
    return pl.BlockSpec(shape, lambda *_: (0,) * nd, pipeline_mode=pl.Buffered(1))


def _mod_kernel(c_ref, w_ref, b_ref, o_ref):
    c = c_ref[...]
    sc = c * _sigmoid(c)
    o_ref[...] = jnp.dot(sc, w_ref[...], preferred_element_type=f32, precision=HIGHEST) + b_ref[...]


def _modulation(c_all, w_ada, b_ada):
    rows = c_all.shape[0]
    tn = 1024
    return pl.pallas_call(
        _mod_kernel,
        out_shape=jax.ShapeDtypeStruct((DEPTH, rows, 6 * D_MODEL), f32),
        grid=(DEPTH, 6 * D_MODEL // tn),
        in_specs=[
            pl.BlockSpec((rows, D_MODEL), lambda l, j: (0, 0)),
            pl.BlockSpec((None, D_MODEL, tn), lambda l, j: (l, 0, j)),
            pl.BlockSpec((None, 1, tn), lambda l, j: (l, 0, j)),
        ],
        out_specs=pl.BlockSpec((None, rows, tn), lambda l, j: (l, 0, j)),
        compiler_params=_params(("parallel", "parallel")),
        name="adaln_modulation",
    )(c_all, w_ada, b_ada.reshape(DEPTH, 1, 6 * D_MODEL))


def _in_proj_kernel(xp_ref, x_ref, xn_ref, mod_ref, g_ref, w_ref, wdt_ref, wdtt_ref, dtb_ref, dtbt_ref, dft_ref,
                    cw_ref, cb_ref, a_ref, sz_ref, xbc_ref, r_ref, gate_ref, dt_ref, dtt_ref, pad_ref, *,
                    tm, tiles_per_seq):
    def modulated(x):
        ms = jnp.mean(x * x, axis=-1, keepdims=True)
        hn = x * lax.rsqrt(ms + EPS) * g_ref[...]
        return hn * (1.0 + mod_ref[1:2, :]) + mod_ref[0:1, :]

    h32 = modulated(x_ref[...])
    h = h32.astype(bf16)

    def proj(c0, c1):
        return jnp.dot(h, w_ref[:, c0:c1], preferred_element_type=f32)

    h_ext = jnp.concatenate([modulated(xp_ref[...]), h32, modulated(xn_ref[...])], axis=0).astype(bf16)
    pad_ref[...] = jnp.dot(h_ext, w_ref[:, _R_XBC:_R_UC], preferred_element_type=f32)
    i = pl.program_id(0)
    first = (i % tiles_per_seq) == 0
    last = (i % tiles_per_seq) == tiles_per_seq - 1
    zeros = jnp.zeros((CONV_S_HALO, XBC_DIM), f32)
    pad_ref[pl.ds(0, CONV_S_HALO), :] = jnp.where(first, zeros, pad_ref[pl.ds(0, CONV_S_HALO), :])
    pad_ref[pl.ds(CONV_S_HALO + tm, CONV_S_HALO), :] = jnp.where(
        last, zeros, pad_ref[pl.ds(CONV_S_HALO + tm, CONV_S_HALO), :])
    rc = 128

    def conv_rows(r0):
        acc = _depthwise(pad_ref, cw_ref, r0 + CONV_S_HALO - SSD_CONV // 2, SSD_CONV, rc) + cb_ref[...]
        xbc_ref[pl.ds(r0, rc), :] = (acc * _sigmoid(acc)).astype(bf16)

    def glu_part():
        a_in = proj(_R_A, _R_Z)
        a_ref[...] = a_in[:, :D_A] * _sigmoid(a_in[:, D_A:])

    def z_part():
        z = proj(_R_Z, _R_XBC)
        sz_ref[...] = (z * _sigmoid(z)).astype(bf16)

    def fourier_part():
        uc = proj(_R_UC, _R_GATE).astype(bf16)
        for g in range(FOURIER_GROUPS):
            lo, hi = g * FOURIER_GROUP_DIM, (g + 1) * FOURIER_GROUP_DIM
            cs = jnp.dot(uc[:, lo:hi], dft_ref[...], preferred_element_type=f32)
            r_ref[0, :, lo:hi] = cs[:, :FOURIER_GROUP_DIM].astype(bf16)
            r_ref[1, :, lo:hi] = cs[:, FOURIER_GROUP_DIM:].astype(bf16)

    def gate_part(j):
        graw = proj(_R_GATE + j * D_MODEL, _R_GATE + (j + 1) * D_MODEL)
        gate_ref[:, j * D_MODEL:(j + 1) * D_MODEL] = _sigmoid(graw).astype(bf16)

    def dt_part():
        dt_ref[...] = _softplus(jnp.dot(h, wdt_ref[...], preferred_element_type=f32) + dtb_ref[...])
        dtt = lax.dot_general(wdtt_ref[...], h, (((1,), (1,)), ((), ())), preferred_element_type=f32)
        dtt_ref[...] = _softplus(dtt + dtbt_ref[...])

    matmul_parts = [glu_part, z_part, fourier_part, functools.partial(gate_part, 0),
                    functools.partial(gate_part, 1), functools.partial(gate_part, 2), dt_part]
    n_conv = tm // rc
    per_conv = -(-len(matmul_parts) // n_conv)
    for k in range(n_conv):
        conv_rows(k * rc)
        for part in matmul_parts[k * per_conv:(k + 1) * per_conv]:
            part()
    for part in matmul_parts[n_conv * per_conv:]:
        part()


def _in_proj(x, mod, g_pre, w_r, w_dt, w_dtt, dtb, dtbt, dft_c, conv_w, conv_b, nb, seq):
    rows = nb * seq
    tm = min(ROW_TILE, seq)
    tiles_per_seq = seq // tm
    row_spec = lambda width: pl.BlockSpec((tm, width), lambda i: (i, 0))
    return pl.pallas_call(
        functools.partial(_in_proj_kernel, tm=tm, tiles_per_seq=tiles_per_seq),
        out_shape=(
            jax.ShapeDtypeStruct((rows, D_A), f32),
            jax.ShapeDtypeStruct((rows, D_SSD), bf16),
            jax.ShapeDtypeStruct((rows, XBC_DIM), bf16),
            jax.ShapeDtypeStruct((2, rows, D_C), bf16),
            jax.ShapeDtypeStruct((rows, 3 * D_MODEL), bf16),
            jax.ShapeDtypeStruct((rows, LANES), f32),
            jax.ShapeDtypeStruct((2 * SSD_HEADS, rows), f32),
        ),
        grid=(rows // tm,),
        in_specs=_halo_specs(tm, CONV_S_HALO, D_MODEL, rows) + [
            pl.BlockSpec((None, 6, D_MODEL), lambda i: (i // tiles_per_seq, 0, 0)),
            _const_spec((1, D_MODEL)),
            _const_spec((D_MODEL, _R_END)),
            _const_spec((D_MODEL, LANES)),
            _const_spec((2 * SSD_HEADS, D_MODEL)),
            _const_spec((1, LANES)),
            _const_spec((2 * SSD_HEADS, 1)),
            _const_spec((FOURIER_GROUP_DIM, 2 * FOURIER_GROUP_DIM)),
            _const_spec((SSD_CONV, XBC_DIM)),
            _const_spec((1, XBC_DIM)),
        ],
        out_specs=(
            row_spec(D_A),
            row_spec(D_SSD),
            row_spec(XBC_DIM),
            pl.BlockSpec((2, tm, D_C), lambda i: (0, i, 0)),
            row_spec(3 * D_MODEL),
            row_spec(LANES),
            pl.BlockSpec((2 * SSD_HEADS, tm), lambda i: (0, i)),
        ),
        scratch_shapes=[pltpu.VMEM((tm + 2 * CONV_S_HALO, XBC_DIM), f32)],
        compiler_params=_params(("parallel",)),
        name="in_proj",
    )(x, x, x, mod, g_pre, w_r, w_dt, w_dtt, dtb, dtbt, dft_c, conv_w, conv_b)


def _fill_padded(pad_ref, prev_ref, main_ref, next_ref, halo, tm, first, last):
    width = main_ref.shape[-1]
    pad_ref[pl.ds(halo, tm), :] = main_ref[...].astype(f32)
    zeros = jnp.zeros((halo, width), f32)
    pad_ref[pl.ds(0, halo), :] = jnp.where(first, zeros, prev_ref[...].astype(f32))
    pad_ref[pl.ds(halo + tm, halo), :] = jnp.where(last, zeros, next_ref[...].astype(f32))


def _depthwise(pad_ref, w_ref, base, taps, rows):
    acc = w_ref[0:1, :] * pad_ref[pl.ds(base, rows), :]
    for k in range(1, taps):
        acc = acc + w_ref[k:k + 1, :] * pad_ref[pl.ds(base + k, rows), :]
    return acc


def _depthwise_aligned(pad_ref, shift_ref, w_ref, base, taps, r0, rows):
    acc = None
    for k in range(taps):
        q, rho = divmod(base + k, SUBLANES)
        src = pad_ref if rho == 0 else shift_ref.at[rho - 1]
        term = w_ref[k:k + 1, :] * src[pl.ds(r0 + q * SUBLANES, rows), :]
        acc = term if acc is None else acc + term
    return acc


def _conv_branch(prev_ref, main_ref, next_ref, w_ref, b_ref, lg_ref, lb_ref, pad_ref, shift_ref, tm, first, last):
    _fill_padded(pad_ref, prev_ref, main_ref, next_ref, CONV_A_HALO, tm, first, last)
    span = tm + 2 * CONV_A_HALO - SUBLANES
    for rho in range(1, SUBLANES):
        shift_ref[rho - 1, pl.ds(0, span), :] = pad_ref[pl.ds(rho, span), :]
    rc = CONV_BLOCK_ROWS
    base = CONV_A_HALO - CONV_A_WIDTH // 2
    pieces = []
    for r0 in range(0, tm, rc):
        acc = _depthwise_aligned(pad_ref, shift_ref, w_ref, base, CONV_A_WIDTH, r0, rc) + b_ref[...]
        mu = jnp.mean(acc, axis=-1, keepdims=True)
        xc = acc - mu
        y = xc * lax.rsqrt(jnp.mean(xc * xc, axis=-1, keepdims=True) + EPS)
        y = y * lg_ref[...] + lb_ref[...]
        pieces.append((y * _sigmoid(y)).astype(bf16))
    return jnp.concatenate(pieces, axis=0)


def _halo_specs(tm, halo, width, n_rows):
    per = tm // halo
    n_blocks = n_rows // halo
    return [
        pl.BlockSpec((halo, width), lambda i: (jnp.maximum(i * per - 1, 0), 0)),
        pl.BlockSpec((tm, width), lambda i: (i, 0)),
        pl.BlockSpec((halo, width), lambda i: (jnp.minimum((i + 1) * per, n_blocks - 1), 0)),
    ]


HALF = CHUNK // 2
GROUP_W = HEADS_PER_GROUP * SSD_HEAD_DIM
N_PAIRS = SSD_HEADS // 2
PAIRS_PER_GROUP = N_PAIRS // SSD_GROUPS
REPLICA = 2 * SSD_HEADS


def _split3(v):
    hi = v.astype(bf16)
    r1 = v - hi.astype(f32)
    mid = r1.astype(bf16)
    lo = (r1 - mid.astype(f32)).astype(bf16)
    return hi, mid, lo


def _pack3(v):
    hi, mid, lo = _split3(v)
    lane = lax.broadcasted_iota(jnp.int32, v.shape, 1)
    zero = jnp.zeros_like(hi)
    return jnp.where(lane < REPLICA, hi, jnp.where(lane < 2 * REPLICA, mid, jnp.where(lane < 3 * REPLICA, lo, zero)))


def _chunk_cumsum_rows(a, tri_l3_ref):
    hi, mid, lo = _split3(a)
    a3 = jnp.concatenate([hi, mid, lo], axis=0)
    return jnp.dot(tri_l3_ref[...], a3, preferred_element_type=f32)


def _decay_compact(a, tri_l3_ref):
    cs = _chunk_cumsum_rows(a, tri_l3_ref)
    tot = cs[CHUNK - 1:CHUNK, :]
    lane = lax.broadcasted_iota(jnp.int32, a.shape, 1)
    is_bwd = (lane % REPLICA) >= SSD_HEADS
    return jnp.where(is_bwd, tot - cs + a, cs), tot


STATE_CHUNKS = 4


def _ssd_state_kernel(xf_ref, dtf_ref, xb_ref, dtb_ref, arow_ref, tri_l3_ref, ef_ref, eb_ref,
                      hf_ref, hb_ref, state_ref):
    @pl.when(pl.program_id(1) == 0)
    def _():
        state_ref[...] = jnp.zeros_like(state_ref)

    lane = lax.broadcasted_iota(jnp.int32, (CHUNK, LANES), 1)
    is_bwd = (lane % REPLICA) >= SSD_HEADS
    steps = []
    for i in range(STATE_CHUNKS):
        rf = pl.ds(i * CHUNK, CHUNK)
        rb = pl.ds((STATE_CHUNKS - 1 - i) * CHUNK, CHUNK)
        dt = jnp.where(is_bwd, dtb_ref[rb, :], dtf_ref[rf, :])
        c, tot = _decay_compact(dt * arow_ref[...], tri_l3_ref)
        w = _pack3(jnp.exp2(tot - c) * dt)
        e_chunk = _pack3(jnp.broadcast_to(jnp.exp2(tot), (SUBLANES, LANES)))
        per_dir = []
        for x_ref, rr, e_ref in ((xf_ref, rf, ef_ref), (xb_ref, rb, eb_ref)):
            w_x = jnp.dot(w, e_ref[...], preferred_element_type=f32)
            decay_x = jnp.dot(e_chunk, e_ref[...], preferred_element_type=f32)[0:1, :]
            states = []
            for g in range(SSD_GROUPS):
                lo = g * GROUP_W
                bm = x_ref[rr, D_SSD + g * SSD_STATE:D_SSD + (g + 1) * SSD_STATE]
                xw = (x_ref[rr, lo:lo + GROUP_W].astype(f32) * w_x[:, lo:lo + GROUP_W]).astype(bf16)
                states.append(lax.dot_general(bm, xw, (((0,), (0,)), ((), ())), preferred_element_type=f32))
            per_dir.append((decay_x, states))
        steps.append(per_dir)
    for i in range(STATE_CHUNKS):
        for d, h_ref, slot in ((0, hf_ref, i), (1, hb_ref, STATE_CHUNKS - 1 - i)):
            decay_x, states = steps[i][d]
            for g in range(SSD_GROUPS):
                lo = g * GROUP_W
                h_in = state_ref[d, g]
                h_ref[slot, g] = h_in.astype(bf16)
                state_ref[d, g] = h_in * decay_x[:, lo:lo + GROUP_W] + states[g]


MAIN_CHUNKS = 2


def _ssd_chunk_kernel(x_ref, dt_ref, dtt_ref, hf_ref, hb_ref, sz_ref, arow_ref, acol_ref, tri_l3_ref, tri_u3_ref,
                      ef_ref, eb_ref, dskip_ref, g_ref, y_ref, rows_ref, acc_ref):
    for i in range(MAIN_CHUNKS):
        rows = pl.ds(i * CHUNK, CHUNK)
        _ssd_one_chunk(x_ref.at[rows, :], dt_ref.at[rows, :], dtt_ref.at[:, rows], hf_ref.at[i], hb_ref.at[i],
                       sz_ref.at[rows, :], arow_ref, acol_ref, tri_l3_ref, tri_u3_ref, ef_ref, eb_ref,
                       dskip_ref, g_ref, y_ref.at[rows, :], rows_ref.at[i], acc_ref.at[i])


def _ssd_one_chunk(x_ref, dt_ref, dtt_ref, hf_ref, hb_ref, sz_ref, arow_ref, acol_ref, tri_l3_ref, tri_u3_ref,
                   ef_ref, eb_ref, dskip_ref, g_ref, y_ref, rows_ref, acc_ref):
    q = CHUNK
    dt = dt_ref[...]
    dtt = dtt_ref[...]
    c, _ = _decay_compact(dt * arow_ref[...], tri_l3_ref)
    at = dtt * acol_ref[...]
    hi, mid, lo = _split3(at)
    cst = jnp.dot(jnp.concatenate([hi, mid, lo], axis=1), tri_u3_ref[...], preferred_element_type=f32)
    row = lax.broadcasted_iota(jnp.int32, at.shape, 0)
    ct = jnp.where(row >= SSD_HEADS, cst[:, q - 1:q] - cst + at, cst)
    ldt = jnp.log2(dtt)

    lane8 = lax.broadcasted_iota(jnp.int32, (N_PAIRS, LANES), 1)
    kind = 0
    for src in (ct - ldt, ldt):
        for d in range(2):
            even = src[d * SSD_HEADS:d * SSD_HEADS + N_PAIRS, :]
            odd = src[d * SSD_HEADS + N_PAIRS:(d + 1) * SSD_HEADS, :]
            rows_ref[pl.ds(kind * N_PAIRS, N_PAIRS), :] = jnp.where(lane8 < HALF, even, pltpu.roll(odd, HALF, 1))
            rows_ref[pl.ds((kind + 1) * N_PAIRS, N_PAIRS), :] = jnp.where(lane8 < HALF, pltpu.roll(even, HALF, 1), odd)
            kind += 2

    c3 = _pack3(c)
    cx = (jnp.dot(c3, ef_ref[...], preferred_element_type=f32),
          jnp.dot(c3, eb_ref[...], preferred_element_type=f32))

    l_in_half = lax.broadcasted_iota(jnp.int32, (HALF, LANES), 0)
    lane_h = lax.broadcasted_iota(jnp.int32, (HALF, LANES), 1)
    s_in_half = lane_h % HALF
    zero_h = jnp.zeros((HALF, LANES), bf16)
    for g in range(SSD_GROUPS):
        bm = x_ref[:, D_SSD + g * SSD_STATE:D_SSD + (g + 1) * SSD_STATE]
        cm = x_ref[:, D_SSD + BC_DIM + g * SSD_STATE:D_SSD + BC_DIM + (g + 1) * SSD_STATE]
        sc = []
        for half in range(2):
            b_half = bm[half * HALF:(half + 1) * HALF, :]
            b_dup = jnp.concatenate([b_half, b_half], axis=0)
            sc.append(lax.dot_general(cm, b_dup, (((1,), (1,)), ((), ())), preferred_element_type=f32))
        low = [jnp.where(l_in_half >= s_in_half, sc[h][h * HALF:(h + 1) * HALF, :], 0.0) for h in range(2)]
        up = [jnp.where(l_in_half <= s_in_half, sc[h][h * HALF:(h + 1) * HALF, :], 0.0) for h in range(2)]
        lo_g = g * GROUP_W
        y_off = (jnp.dot(cm, hf_ref[g], preferred_element_type=f32) * jnp.exp2(cx[0][:, lo_g:lo_g + GROUP_W])
                 + jnp.dot(cm, hb_ref[g], preferred_element_type=f32) * jnp.exp2(cx[1][:, lo_g:lo_g + GROUP_W]))
        acc_ref[:, lo_g:lo_g + GROUP_W] = y_off
        for jj in range(PAIRS_PER_GROUP):
            j = g * PAIRS_PER_GROUP + jj
            lo_j = j * LANES
            row_vec = lambda k: rows_ref[pl.ds(k * N_PAIRS + j, 1), :]
            quads = [[None, None], [None, None]]
            for th in range(2):
                rows_t = slice(th * HALF, (th + 1) * HALF)
                for sh in range(2):
                    if th == sh:
                        seg_f = jnp.minimum(cx[0][rows_t, lo_j:lo_j + LANES] - row_vec(0 + sh), row_vec(4 + sh))
                        seg_b = jnp.minimum(cx[1][rows_t, lo_j:lo_j + LANES] - row_vec(2 + sh), row_vec(6 + sh))
                        t = low[sh] * jnp.exp2(seg_f) + up[sh] * jnp.exp2(seg_b)
                    else:
                        d = 0 if th > sh else 1
                        seg = cx[d][rows_t, lo_j:lo_j + LANES] - row_vec(2 * d + sh)
                        t = sc[sh][rows_t, :] * jnp.exp2(seg)
                    quads[th][sh] = t.astype(bf16)
            lhs = jnp.concatenate([jnp.concatenate(quads[0], axis=1), jnp.concatenate(quads[1], axis=1)], axis=0)
            xp = x_ref[:, lo_j:lo_j + LANES]
            blocks = []
            for sh in range(2):
                xh = xp[sh * HALF:(sh + 1) * HALF, :]
                blocks.append(jnp.where(lane_h < HALF, xh, zero_h))
                blocks.append(jnp.where(lane_h >= HALF, xh, zero_h))
            rhs = jnp.concatenate(blocks, axis=0)
            acc_ref[:, lo_j:lo_j + LANES] += jnp.dot(lhs, rhs, preferred_element_type=f32)

    y = acc_ref[...] + dskip_ref[...] * x_ref[:, 0:D_SSD].astype(f32)
    y = y * sz_ref[...].astype(f32)
    y = y * lax.rsqrt(jnp.mean(y * y, axis=-1, keepdims=True) + EPS) * g_ref[...]
    y_ref[...] = y.astype(bf16)


def _ssd(xbc_c, dt, dtt, sz, a_row, a_col, consts, dskip_lanes, g_ssd, nb, seq):
    rows = nb * seq
    nc = seq // CHUNK
    state_shape = (SSD_GROUPS, SSD_STATE, GROUP_W)
    assert nc % STATE_CHUNKS == 0
    nblk = nc // STATE_CHUNKS
    blk_rows = STATE_CHUNKS * CHUNK
    x_spec = lambda pos: pl.BlockSpec((blk_rows, XBC_DIM), lambda b, k: (b * nblk + pos(k), 0))
    dt_spec = lambda pos: pl.BlockSpec((blk_rows, LANES), lambda b, k: (b * nblk + pos(k), 0))
    h_spec = lambda pos: pl.BlockSpec((STATE_CHUNKS,) + state_shape, lambda b, k: (b * nblk + pos(k), 0, 0, 0))
    fwd = lambda k: k
    rev = lambda k: nblk - 1 - k
    h_shape = jax.ShapeDtypeStruct((nb * nc,) + state_shape, bf16)
    h_f, h_b = pl.pallas_call(
        _ssd_state_kernel,
        out_shape=(h_shape, h_shape),
        grid=(nb, nblk),
        in_specs=[x_spec(fwd), dt_spec(fwd), x_spec(rev), dt_spec(rev),
                  _const_spec((1, LANES)), _const_spec((CHUNK, 3 * CHUNK)),
                  _const_spec((LANES, D_SSD)), _const_spec((LANES, D_SSD))],
        out_specs=(h_spec(fwd), h_spec(rev)),
        scratch_shapes=[pltpu.VMEM((2,) + state_shape, f32)],
        compiler_params=_params(("parallel", "arbitrary")),
        name="ssd_state_scan",
    )(xbc_c, dt, xbc_c, dt, a_row, consts["tri_l3"], consts["expand_f"], consts["expand_b"])

    assert nc % MAIN_CHUNKS == 0
    step_rows = MAIN_CHUNKS * CHUNK
    row_spec = lambda width: pl.BlockSpec((step_rows, width), lambda i: (i, 0))
    hc_spec = pl.BlockSpec((MAIN_CHUNKS,) + state_shape, lambda i: (i, 0, 0, 0))
    return pl.pallas_call(
        _ssd_chunk_kernel,
        out_shape=jax.ShapeDtypeStruct((rows, D_SSD), bf16),
        grid=(nb * nc // MAIN_CHUNKS,),
        in_specs=[row_spec(XBC_DIM), row_spec(LANES), pl.BlockSpec((2 * SSD_HEADS, step_rows), lambda i: (0, i)),
                  hc_spec, hc_spec, row_spec(D_SSD),
                  _const_spec((1, LANES)), _const_spec((2 * SSD_HEADS, 1)),
                  _const_spec((CHUNK, 3 * CHUNK)), _const_spec((3 * CHUNK, CHUNK)),
                  _const_spec((LANES, D_SSD)), _const_spec((LANES, D_SSD)),
                  _const_spec((1, D_SSD)), _const_spec((1, D_SSD))],
        out_specs=row_spec(D_SSD),
        scratch_shapes=[pltpu.VMEM((MAIN_CHUNKS, 8 * N_PAIRS, LANES), f32),
                        pltpu.VMEM((MAIN_CHUNKS, CHUNK, D_SSD), f32)],
        compiler_params=_params(("parallel",)),
        name="ssd_chunk",
    )(xbc_c, dt, dtt, h_f, h_b, sz, a_row, a_col, consts["tri_l3"], consts["tri_u3"],
      consts["expand_f"], consts["expand_b"], dskip_lanes, g_ssd)


DFT_L2 = 64
DFT_COLS = 16


def _dft_stage1_kernel(r_ref, m_ref, tc_ref, ts_ref, o_ref, *, l1):
    x = jnp.dot(m_ref[...], r_ref[...].reshape(2 * l1, DFT_COLS * D_C), preferred_element_type=f32)
    xr, xi = x[:l1, :], x[l1:, :]
    for j in range(DFT_COLS):
        tc = tc_ref[:, j * LANES:(j + 1) * LANES]
        ts = ts_ref[:, j * LANES:(j + 1) * LANES]
        tc = jnp.concatenate([tc] * (D_C // LANES), axis=1)
        ts = jnp.concatenate([ts] * (D_C // LANES), axis=1)
        ar, ai = xr[:, j * D_C:(j + 1) * D_C], xi[:, j * D_C:(j + 1) * D_C]
        o_ref[0, j] = (ar * tc + ai * ts).astype(bf16)
        o_ref[1, j] = (ai * tc - ar * ts).astype(bf16)


def _dft_stage2_kernel(a_ref, m_ref, o_ref, *, scale):
    x = jnp.dot(m_ref[...], a_ref[...].reshape(2 * DFT_L2, DFT_COLS * D_C), preferred_element_type=f32)
    o_ref[...] = (x * scale).astype(bf16)


def _seq_dft(r, fc, nb, seq):
    l1, l2 = seq // DFT_L2, DFT_L2
    stage1 = pl.pallas_call(
        functools.partial(_dft_stage1_kernel, l1=l1),
        out_shape=jax.ShapeDtypeStruct((2, nb, l2, l1, D_C), bf16),
        grid=(nb, l2 // DFT_COLS),
        in_specs=[pl.BlockSpec((2, None, l1, DFT_COLS * D_C), lambda b, j: (0, b, 0, j)),
                  _const_spec((2 * l1, 2 * l1)),
                  pl.BlockSpec((l1, DFT_COLS * LANES), lambda b, j: (0, j)),
                  pl.BlockSpec((l1, DFT_COLS * LANES), lambda b, j: (0, j))],
        out_specs=pl.BlockSpec((2, None, DFT_COLS, l1, D_C), lambda b, j: (0, b, j, 0, 0)),
        compiler_params=_params(("parallel", "parallel")),
        name="fourier_dft_stage1",
    )(r.reshape(2, nb, l1, l2 * D_C), fc["m1"], fc["tc"], fc["ts"])
    scale = 1.0 / math.sqrt(seq * FOURIER_GROUP_DIM)
    out = pl.pallas_call(
        functools.partial(_dft_stage2_kernel, scale=scale),
        out_shape=jax.ShapeDtypeStruct((nb, l2, l1 * D_C), bf16),
        grid=(nb, l1 // DFT_COLS),
        in_specs=[pl.BlockSpec((2, None, l2, DFT_COLS * D_C), lambda b, j: (0, b, 0, j)),
                  _const_spec((l2, 2 * l2))],
        out_specs=pl.BlockSpec((None, l2, DFT_COLS * D_C), lambda b, j: (b, 0, j)),
        compiler_params=_params(("parallel", "parallel")),
        name="fourier_dft_stage2",
    )(stage1.reshape(2, nb, l2, l1 * D_C), fc["m2"])
    return out.reshape(nb * seq, D_C)


def _seq_dft_consts(seq):
    l1, l2 = seq // DFT_L2, DFT_L2
    c1, s1 = _dft_cos_sin(l1)
    c2, s2 = _dft_cos_sin(l2)
    k1 = lax.broadcasted_iota(jnp.int32, (l1, l2), 0)
    n2 = lax.broadcasted_iota(jnp.int32, (l1, l2), 1)
    ang = ((k1 * n2) % seq).astype(f32) * (2.0 * math.pi / seq)
    lanes = lambda t: jnp.repeat(t, LANES, axis=1)
    return {
        "m1": jnp.concatenate([jnp.concatenate([c1, -s1], axis=1),
                               jnp.concatenate([-s1, -c1], axis=1)], axis=0).astype(bf16),
        "m2": jnp.concatenate([c2, s2], axis=1).astype(bf16),
        "tc": lanes(jnp.cos(ang)), "ts": lanes(jnp.sin(ang)),
    }


FFN_CHUNK = D_FF // 2
POST_TILE = 256


def _post_kernel(ap_ref, a_ref, an_ref, x_ref, mod_ref, yb_ref, fc_ref, gate_ref,
                 cw_ref, cb_ref, lg_ref, lb_ref, wa_ref, wb_ref, wc_ref, wo_ref, gmix_ref,
                 gpre_ref, wi_ref, wf_ref, gffn_ref, o_ref, pad_ref, shift_ref, *, tm, tiles_per_seq):
    i = pl.program_id(0)
    first = (i % tiles_per_seq) == 0
    last = (i % tiles_per_seq) == tiles_per_seq - 1
    a2 = _conv_branch(ap_ref, a_ref, an_ref, cw_ref, cb_ref, lg_ref, lb_ref, pad_ref, shift_ref, tm, first, last)
    o_a = jnp.dot(a2, wa_ref[...], preferred_element_type=f32)
    merged = gate_ref[:, 0:D_MODEL].astype(f32) * o_a
    o_b = jnp.dot(yb_ref[...], wb_ref[...], preferred_element_type=f32)
    merged = merged + gate_ref[:, D_MODEL:2 * D_MODEL].astype(f32) * o_b
    o_c = jnp.dot(fc_ref[...], wc_ref[...], preferred_element_type=f32)
    merged = merged + gate_ref[:, 2 * D_MODEL:3 * D_MODEL].astype(f32) * o_c
    m = jnp.dot(merged.astype(bf16), wo_ref[...], preferred_element_type=f32)
    mn = m * lax.rsqrt(jnp.mean(m * m, axis=-1, keepdims=True) + EPS) * gmix_ref[...]
    x = x_ref[...] + mod_ref[2:3, :] * mn

    hn = x * lax.rsqrt(jnp.mean(x * x, axis=-1, keepdims=True) + EPS) * gpre_ref[...]
    h = (hn * (1.0 + mod_ref[4:5, :]) + mod_ref[3:4, :]).astype(bf16)
    f = None
    for c in range(D_FF // FFN_CHUNK):
        lo = c * FFN_CHUNK
        gate = jnp.dot(h, wi_ref[:, lo:lo + FFN_CHUNK], preferred_element_type=f32)
        up = jnp.dot(h, wi_ref[:, D_FF + lo:D_FF + lo + FFN_CHUNK], preferred_element_type=f32)
        act = (gate * _sigmoid(gate) * up).astype(bf16)
        part = jnp.dot(act, wf_ref[lo:lo + FFN_CHUNK, :], preferred_element_type=f32)
        f = part if f is None else f + part
    fn = f * lax.rsqrt(jnp.mean(f * f, axis=-1, keepdims=True) + EPS) * gffn_ref[...]
    o_ref[...] = x + mod_ref[5:6, :] * fn


def _post(a, x, mod, yb, fc, gates, lw, nb, seq):
    rows = nb * seq
    tm = min(POST_TILE, seq)
    tiles_per_seq = seq // tm
    row_spec = lambda width: pl.BlockSpec((tm, width), lambda i: (i, 0))
    vec = lambda width: _const_spec((1, width))
    return pl.pallas_call(
        functools.partial(_post_kernel, tm=tm, tiles_per_seq=tiles_per_seq),
        out_shape=jax.ShapeDtypeStruct((rows, D_MODEL), f32),
        grid=(rows // tm,),
        in_specs=_halo_specs(tm, CONV_A_HALO, D_A, rows) + [
            row_spec(D_MODEL),
            pl.BlockSpec((None, 6, D_MODEL), lambda i: (i // tiles_per_seq, 0, 0)),
            row_spec(D_SSD),
            row_spec(D_C),
            row_spec(3 * D_MODEL),
            _const_spec((CONV_A_WIDTH, D_A)), vec(D_A), vec(D_A), vec(D_A),
            _const_spec((D_A, D_MODEL)), _const_spec((D_SSD, D_MODEL)), _const_spec((D_C, D_MODEL)),
            _const_spec((D_MODEL, D_MODEL)), vec(D_MODEL),
            vec(D_MODEL), _const_spec((D_MODEL, 2 * D_FF)), _const_spec((D_FF, D_MODEL)), vec(D_MODEL),
        ],
        out_specs=row_spec(D_MODEL),
        scratch_shapes=[pltpu.VMEM((tm + 2 * CONV_A_HALO, D_A), f32),
                        pltpu.VMEM((SUBLANES - 1, tm + 2 * CONV_A_HALO, D_A), f32)],
        compiler_params=_params(("parallel",)),
        name="conv_merge_ffn",
    )(a, a, a, x, mod, yb, fc, gates,
      lw["conv_a_w"], lw["conv_a_b"], lw["ln_a_g"], lw["ln_a_b"],
      lw["w_a_out"], lw["w_b_out"], lw["w_c_out"], lw["w_out"], lw["g_post_mix"],
      lw["g_pre_ffn"], lw["w_ffn_in"], lw["w_ffn_out"], lw["g_post_ffn"])


def _dft_cos_sin(n):
    j = lax.broadcasted_iota(jnp.int32, (n, n), 0)
    k = lax.broadcasted_iota(jnp.int32, (n, n), 1)
    ang = ((j * k) % n).astype(f32) * (2.0 * math.pi / n)
    return jnp.cos(ang), jnp.sin(ang)


def _layer(x, mod, lw, consts, nb, seq):
    a, sz, xbc_c, r, gates, dt, dtt = _in_proj(
        x, mod, lw["g_pre_mix"], lw["w_r"], lw["w_dt"], lw["w_dtt"], lw["dtb"], lw["dtbt"], consts["dft_c"],
        lw["conv_s_w"], lw["conv_s_b"], nb, seq)
    yb = _ssd(xbc_c, dt, dtt, sz, lw["a_row"], lw["a_col"], consts, lw["dskip"], lw["g_ssd"], nb, seq)
    fc = _seq_dft(r, consts["dft_l"][seq], nb, seq)
    return _post(a, x, mod, yb, fc, gates, lw, nb, seq)


def kernel(x_prompt, x_sample, c_prompt, c_sample, w_ada, b_ada, g_pre_mix, g_post_mix, g_pre_ffn, g_post_ffn, w_in, conv_a_w, conv_a_b, ln_a_g, ln_a_b, w_a_out, conv_s_w, conv_s_b, dt_bias_f, dt_bias_b, a_log_f, a_log_b, d_skip, g_ssd, w_b_out, w_c_out, w_out, w_ffn_in, w_ffn_out):
    depth = w_in.shape[0]
    assert depth == DEPTH
    bp, lp, _ = x_prompt.shape
    bs, ls, _ = x_sample.shape

    n_seq = bp + bs
    pad_rows = -n_seq % SUBLANES
    c_all = jnp.concatenate([c_prompt, c_sample, jnp.zeros((pad_rows, D_MODEL), f32)], axis=0)
    mod_all = _modulation(c_all, w_ada, b_ada).reshape(depth, n_seq + pad_rows, 6, D_MODEL)

    w_r = jnp.concatenate([w_in[:, :, _OFF_A:_OFF_DT], w_in[:, :, _OFF_UC:_OFF_END]], axis=-1).astype(bf16)
    reps = LANES // REPLICA
    t_order = jnp.array([d * SSD_HEADS + 2 * j + par for d in range(2) for par in range(2) for j in range(N_PAIRS)])
    w_dt_raw = w_in[:, :, _OFF_DT:_OFF_UC]
    w_dt = jnp.tile(w_dt_raw, (1, 1, reps)).astype(bf16)
    w_dtt = jnp.swapaxes(w_dt_raw[:, :, t_order], 1, 2).astype(bf16)
    dtb_cat = jnp.concatenate([dt_bias_f, dt_bias_b], axis=-1)
    dtb = jnp.tile(dtb_cat, (1, reps))
    a_cat = jnp.concatenate([-jnp.exp(a_log_f), -jnp.exp(a_log_b)], axis=-1) * math.log2(math.e)
    a_row = jnp.tile(a_cat, (1, reps))

    cos_c, sin_c = _dft_cos_sin(FOURIER_GROUP_DIM)
    t_i = lax.broadcasted_iota(jnp.int32, (CHUNK, CHUNK), 0)
    u_i = lax.broadcasted_iota(jnp.int32, (CHUNK, CHUNK), 1)
    tri = (u_i <= t_i).astype(bf16)
    src_lane = lax.broadcasted_iota(jnp.int32, (LANES, D_SSD), 0)
    dst_head = lax.broadcasted_iota(jnp.int32, (LANES, D_SSD), 1) // SSD_HEAD_DIM
    live = src_lane < 3 * REPLICA
    consts = {
        "dft_c": jnp.concatenate([cos_c, sin_c], axis=1).astype(bf16),
        "tri_l3": jnp.tile(tri, (1, 3)),
        "tri_u3": jnp.tile(tri.T, (3, 1)),
        "expand_f": (live & (src_lane % REPLICA == dst_head)).astype(bf16),
        "expand_b": (live & (src_lane % REPLICA == dst_head + SSD_HEADS)).astype(bf16),
        "dft_l": {},
    }
    for seq in sorted({lp, ls}):
        consts["dft_l"][seq] = _seq_dft_consts(seq)

    row = lambda v: v.reshape(1, -1)
    y_prompt = x_prompt.reshape(bp * lp, D_MODEL)
    y_sample = x_sample.reshape(bs * ls, D_MODEL)
    for i in range(depth):
        lw = {
            "g_pre_mix": row(g_pre_mix[i]), "g_post_mix": row(g_post_mix[i]),
            "g_pre_ffn": row(g_pre_ffn[i]), "g_post_ffn": row(g_post_ffn[i]),
            "w_r": w_r[i], "w_dt": w_dt[i], "w_dtt": w_dtt[i],
            "dtb": row(dtb[i]), "dtbt": dtb_cat[i][t_order].reshape(-1, 1),
            "conv_a_w": conv_a_w[i], "conv_a_b": row(conv_a_b[i]), "ln_a_g": row(ln_a_g[i]), "ln_a_b": row(ln_a_b[i]),
            "conv_s_w": conv_s_w[i], "conv_s_b": row(conv_s_b[i]),
            "a_row": row(a_row[i]), "a_col": a_cat[i][t_order].reshape(-1, 1),
            "dskip": row(jnp.repeat(d_skip[i], SSD_HEAD_DIM)), "g_ssd": row(g_ssd[i]),
            "w_a_out": w_a_out[i].astype(bf16), "w_b_out": w_b_out[i].astype(bf16),
            "w_c_out": w_c_out[i].astype(bf16), "w_out": w_out[i].astype(bf16),
            "w_ffn_in": w_ffn_in[i].astype(bf16), "w_ffn_out": w_ffn_out[i].astype(bf16),
        }
        y_prompt = _layer(y_prompt, mod_all[i, :bp], lw, consts, bp, lp)
        y_sample = _layer(y_sample, mod_all[i, bp:bp + bs], lw, consts, bs, ls)
    return (y_prompt.reshape(bp, lp, D_MODEL), y_sample.reshape(bs, ls, D_MODEL))
```

```python
import functools
import math

import jax
import jax.numpy as jnp
from jax import lax
from jax.experimental import pallas as pl
from jax.experimental.pallas import tpu as pltpu

f32 = jnp.float32
bf16 = jnp.bfloat16
HIGHEST = lax.Precision.HIGHEST

D_MODEL = 1024
DEPTH = 4
D_A = 512
CONV_A_WIDTH = 31
SSD_HEADS = 16
SSD_HEAD_DIM = 64
D_SSD = SSD_HEADS * SSD_HEAD_DIM
SSD_GROUPS = 2
HEADS_PER_GROUP = SSD_HEADS // SSD_GROUPS
SSD_STATE = 128
SSD_CONV = 5
CHUNK = 128
BC_DIM = SSD_GROUPS * SSD_STATE
XBC_DIM = D_SSD + 2 * BC_DIM
D_C = 512
FOURIER_GROUPS = 4
FOURIER_GROUP_DIM = 128
D_FF = 2816
EPS = 1e-6

_OFF_A, _OFF_Z, _OFF_XBC, _OFF_DT, _OFF_UC, _OFF_GATE, _OFF_END = 0, 1024, 2048, 3584, 3616, 4128, 7200
_R_A, _R_Z, _R_XBC, _R_UC, _R_GATE, _R_END = 0, 1024, 2048, 3584, 4096, 7168

LANES = 128
SUBLANES = 8
VMEM_LIMIT_BYTES = 58 * 1024 * 1024

ROW_TILE = 512
CONV_A_HALO = 16
CONV_S_HALO = 8
CONV_BLOCK_ROWS = 16


def _sigmoid(x):
    return 1.0 / (1.0 + jnp.exp(-x))


def _softplus(x):
    return jnp.maximum(x, 0.0) + jnp.log1p(jnp.exp(-jnp.abs(x)))


def _params(semantics):
    return pltpu.CompilerParams(dimension_semantics=semantics, vmem_limit_bytes=VMEM_LIMIT_BYTES)


def _const_spec(shape):
    nd = len(shape)
    return pl.BlockSpec(shape, lambda *_: (0,) * nd, pipeline_mode=pl.Buffered(1))


def _mod_kernel(c_ref, w_ref, b_ref, o_ref):
    c = c_ref[...]
    sc = c * _sigmoid(c)
    o_ref[...] = jnp.dot(sc, w_ref[...], preferred_element_type=f32, precision=HIGHEST) + b_ref[...]


def _modulation(c_all, w_ada, b_ada):
    rows = c_all.shape[0]
    tn = 1024
    return pl.pallas_call(
        _mod_kernel,
        out_shape=jax.ShapeDtypeStruct((DEPTH, rows, 6 * D_MODEL), f32),
        grid=(DEPTH, 6 * D_MODEL // tn),
        in_specs=[
            pl.BlockSpec((rows, D_MODEL), lambda l, j: (0, 0)),
            pl.BlockSpec((None, D_MODEL, tn), lambda l, j: (l, 0, j)),
            pl.BlockSpec((None, 1, tn), lambda l, j: (l, 0, j)),
        ],
        out_specs=pl.BlockSpec((None, rows, tn), lambda l, j: (l, 0, j)),
        compiler_params=_params(("parallel", "parallel")),
        name="adaln_modulation",
    )(c_all, w_ada, b_ada.reshape(DEPTH, 1, 6 * D_MODEL))


def _in_proj_kernel(xp_ref, x_ref, xn_ref, mod_ref, g_ref, w_ref, wdt_ref, wdtt_ref, dtb_ref, dtbt_ref, dft_ref,
                    cw_ref, cb_ref, a_ref, sz_ref, xbc_ref, r_ref, gate_ref, dt_ref, dtt_ref, pad_ref, *,
                    tm, tiles_per_seq):
    def modulated(x):
        ms = jnp.mean(x * x, axis=-1, keepdims=True)
        hn = x * lax.rsqrt(ms + EPS) * g_ref[...]
        return hn * (1.0 + mod_ref[1:2, :]) + mod_ref[0:1, :]

    h32 = modulated(x_ref[...])
    h = h32.astype(bf16)

    def proj(c0, c1):
        return jnp.dot(h, w_ref[:, c0:c1], preferred_element_type=f32)

    h_ext = jnp.concatenate([modulated(xp_ref[...]), h32, modulated(xn_ref[...])], axis=0).astype(bf16)
    pad_ref[...] = jnp.dot(h_ext, w_ref[:, _R_XBC:_R_UC], preferred_element_type=f32)
    i = pl.program_id(0)
    first = (i % tiles_per_seq) == 0
    last = (i % tiles_per_seq) == tiles_per_seq - 1
    zeros = jnp.zeros((CONV_S_HALO, XBC_DIM), f32)
    pad_ref[pl.ds(0, CONV_S_HALO), :] = jnp.where(first, zeros, pad_ref[pl.ds(0, CONV_S_HALO), :])
    pad_ref[pl.ds(CONV_S_HALO + tm, CONV_S_HALO), :] = jnp.where(
        last, zeros, pad_ref[pl.ds(CONV_S_HALO + tm, CONV_S_HALO), :])
    rc = 128

    def conv_rows(r0):
        acc = _depthwise(pad_ref, cw_ref, r0 + CONV_S_HALO - SSD_CONV // 2, SSD_CONV, rc) + cb_ref[...]
        xbc_ref[pl.ds(r0, rc), :] = (acc * _sigmoid(acc)).astype(bf16)

    def glu_part():
        a_in = proj(_R_A, _R_Z)
        a_ref[...] = a_in[:, :D_A] * _sigmoid(a_in[:, D_A:])

    def z_part():
        z = proj(_R_Z, _R_XBC)
        sz_ref[...] = (z * _sigmoid(z)).astype(bf16)

    def fourier_part():
        uc = proj(_R_UC, _R_GATE).astype(bf16)
        for g in range(FOURIER_GROUPS):
            lo, hi = g * FOURIER_GROUP_DIM, (g + 1) * FOURIER_GROUP_DIM
            cs = jnp.dot(uc[:, lo:hi], dft_ref[...], preferred_element_type=f32)
            r_ref[0, :, lo:hi] = cs[:, :FOURIER_GROUP_DIM].astype(bf16)
            r_ref[1, :, lo:hi] = cs[:, FOURIER_GROUP_DIM:].astype(bf16)

    def gate_part(j):
        graw = proj(_R_GATE + j * D_MODEL, _R_GATE + (j + 1) * D_MODEL)
        gate_ref[:, j * D_MODEL:(j + 1) * D_MODEL] = _sigmoid(graw).astype(bf16)

    def dt_part():
        dt_ref[...] = _softplus(jnp.dot(h, wdt_ref[...], preferred_element_type=f32) + dtb_ref[...])
        dtt = lax.dot_general(wdtt_ref[...], h, (((1,), (1,)), ((), ())), preferred_element_type=f32)
        dtt_ref[...] = _softplus(dtt + dtbt_ref[...])

    matmul_parts = [glu_part, z_part, fourier_part, functools.partial(gate_part, 0),
                    functools.partial(gate_part, 1), functools.partial(gate_part, 2), dt_part]
    n_conv = tm // rc
    per_conv = -(-len(matmul_parts) // n_conv)
    for k in range(n_conv):
        conv_rows(k * rc)
        for part in matmul_parts[k * per_conv:(k + 1) * per_conv]:
            part()
    for part in matmul_parts[n_conv * per_conv:]:
        part()


def _in_proj(x, mod, g_pre, w_r, w_dt, w_dtt, dtb, dtbt, dft_c, conv_w, conv_b, nb, seq):
    rows = nb * seq
    tm = min(ROW_TILE, seq)
    tiles_per_seq = seq // tm
    row_spec = lambda width: pl.BlockSpec((tm, width), lambda i: (i, 0))
    return pl.pallas_call(
        functools.partial(_in_proj_kernel, tm=tm, tiles_per_seq=tiles_per_seq),
        out_shape=(
            jax.ShapeDtypeStruct((rows, D_A), f32),
            jax.ShapeDtypeStruct((rows, D_SSD), bf16),
            jax.ShapeDtypeStruct((rows, XBC_DIM), bf16),
            jax.ShapeDtypeStruct((2, seq, nb * D_C), bf16),
            jax.ShapeDtypeStruct((rows, 3 * D_MODEL), bf16),
            jax.ShapeDtypeStruct((rows, LANES), f32),
            jax.ShapeDtypeStruct((2 * SSD_HEADS, rows), f32),
        ),
        grid=(rows // tm,),
        in_specs=_halo_specs(tm, CONV_S_HALO, D_MODEL, rows) + [
            pl.BlockSpec((None, 6, D_MODEL), lambda i: (i // tiles_per_seq, 0, 0)),
            _const_spec((1, D_MODEL)),
            _const_spec((D_MODEL, _R_END)),
            _const_spec((D_MODEL, LANES)),
            _const_spec((2 * SSD_HEADS, D_MODEL)),
            _const_spec((1, LANES)),
            _const_spec((2 * SSD_HEADS, 1)),
            _const_spec((FOURIER_GROUP_DIM, 2 * FOURIER_GROUP_DIM)),
            _const_spec((SSD_CONV, XBC_DIM)),
            _const_spec((1, XBC_DIM)),
        ],
        out_specs=(
            row_spec(D_A),
            row_spec(D_SSD),
            row_spec(XBC_DIM),
            pl.BlockSpec((2, tm, D_C), lambda i: (0, i % tiles_per_seq, i // tiles_per_seq)),
            row_spec(3 * D_MODEL),
            row_spec(LANES),
            pl.BlockSpec((2 * SSD_HEADS, tm), lambda i: (0, i)),
        ),
        scratch_shapes=[pltpu.VMEM((tm + 2 * CONV_S_HALO, XBC_DIM), f32)],
        compiler_params=_params(("parallel",)),
        name="in_proj",
    )(x, x, x, mod, g_pre, w_r, w_dt, w_dtt, dtb, dtbt, dft_c, conv_w, conv_b)


def _fill_padded(pad_ref, prev_ref, main_ref, next_ref, halo, tm, first, last):
    width = main_ref.shape[-1]
    pad_ref[pl.ds(halo, tm), :] = main_ref[...].astype(f32)
    zeros = jnp.zeros((halo, width), f32)
    pad_ref[pl.ds(0, halo), :] = jnp.where(first, zeros, prev_ref[...].astype(f32))
    pad_ref[pl.ds(halo + tm, halo), :] = jnp.where(last, zeros, next_ref[...].astype(f32))


def _depthwise(pad_ref, w_ref, base, taps, rows):
    acc = w_ref[0:1, :] * pad_ref[pl.ds(base, rows), :]
    for k in range(1, taps):
        acc = acc + w_ref[k:k + 1, :] * pad_ref[pl.ds(base + k, rows), :]
    return acc


def _depthwise_aligned(pad_ref, shift_ref, w_ref, base, taps, r0, rows):
    acc = None
    for k in range(taps):
        q, rho = divmod(base + k, SUBLANES)
        src = pad_ref if rho == 0 else shift_ref.at[rho - 1]
        term = w_ref[k:k + 1, :] * src[pl.ds(r0 + q * SUBLANES, rows), :]
        acc = term if acc is None else acc + term
    return acc


def _conv_branch(prev_ref, main_ref, next_ref, w_ref, b_ref, lg_ref, lb_ref, pad_ref, shift_ref, tm, first, last):
    _fill_padded(pad_ref, prev_ref, main_ref, next_ref, CONV_A_HALO, tm, first, last)
    span = tm + 2 * CONV_A_HALO - SUBLANES
    for rho in range(1, SUBLANES):
        shift_ref[rho - 1, pl.ds(0, span), :] = pad_ref[pl.ds(rho, span), :]
    rc = CONV_BLOCK_ROWS
    base = CONV_A_HALO - CONV_A_WIDTH // 2
    pieces = []
    for r0 in range(0, tm, rc):
        acc = _depthwise_aligned(pad_ref, shift_ref, w_ref, base, CONV_A_WIDTH, r0, rc) + b_ref[...]
        mu = jnp.mean(acc, axis=-1, keepdims=True)
        xc = acc - mu
        y = xc * lax.rsqrt(jnp.mean(xc * xc, axis=-1, keepdims=True) + EPS)
        y = y * lg_ref[...] + lb_ref[...]
        pieces.append((y * _sigmoid(y)).astype(bf16))
    return jnp.concatenate(pieces, axis=0)


def _halo_specs(tm, halo, width, n_rows):
    per = tm // halo
    n_blocks = n_rows // halo
    return [
        pl.BlockSpec((halo, width), lambda i: (jnp.maximum(i * per - 1, 0), 0)),
        pl.BlockSpec((tm, width), lambda i: (i, 0)),
        pl.BlockSpec((halo, width), lambda i: (jnp.minimum((i + 1) * per, n_blocks - 1), 0)),
    ]


HALF = CHUNK // 2
GROUP_W = HEADS_PER_GROUP * SSD_HEAD_DIM
N_PAIRS = SSD_HEADS // 2
PAIRS_PER_GROUP = N_PAIRS // SSD_GROUPS
REPLICA = 2 * SSD_HEADS


def _split3(v):
    hi = v.astype(bf16)
    r1 = v - hi.astype(f32)
    mid = r1.astype(bf16)
    lo = (r1 - mid.astype(f32)).astype(bf16)
    return hi, mid, lo


def _pack3(v):
    hi, mid, lo = _split3(v)
    lane = lax.broadcasted_iota(jnp.int32, v.shape, 1)
    zero = jnp.zeros_like(hi)
    return jnp.where(lane < REPLICA, hi, jnp.where(lane < 2 * REPLICA, mid, jnp.where(lane < 3 * REPLICA, lo, zero)))


def _chunk_cumsum_rows(a, tri_l3_ref):
    hi, mid, lo = _split3(a)
    a3 = jnp.concatenate([hi, mid, lo], axis=0)
    return jnp.dot(tri_l3_ref[...], a3, preferred_element_type=f32)


def _decay_compact(a, tri_l3_ref):
    cs = _chunk_cumsum_rows(a, tri_l3_ref)
    tot = cs[CHUNK - 1:CHUNK, :]
    lane = lax.broadcasted_iota(jnp.int32, a.shape, 1)
    is_bwd = (lane % REPLICA) >= SSD_HEADS
    return jnp.where(is_bwd, tot - cs + a, cs), tot


STATE_CHUNKS = 8


def _ssd_state_kernel(xf_ref, dtf_ref, xb_ref, dtb_ref, arow_ref, tri_l3_ref, ef_ref, eb_ref,
                      hf_ref, hb_ref, state_ref):
    @pl.when(pl.program_id(1) == 0)
    def _():
        state_ref[...] = jnp.zeros_like(state_ref)

    lane = lax.broadcasted_iota(jnp.int32, (CHUNK, LANES), 1)
    is_bwd = (lane % REPLICA) >= SSD_HEADS
    steps = []
    for i in range(STATE_CHUNKS):
        rf = pl.ds(i * CHUNK, CHUNK)
        rb = pl.ds((STATE_CHUNKS - 1 - i) * CHUNK, CHUNK)
        dt = jnp.where(is_bwd, dtb_ref[rb, :], dtf_ref[rf, :])
        c, tot = _decay_compact(dt * arow_ref[...], tri_l3_ref)
        w = _pack3(jnp.exp2(tot - c) * dt)
        e_chunk = _pack3(jnp.broadcast_to(jnp.exp2(tot), (SUBLANES, LANES)))
        per_dir = []
        for x_ref, rr, e_ref in ((xf_ref, rf, ef_ref), (xb_ref, rb, eb_ref)):
            w_x = jnp.dot(w, e_ref[...], preferred_element_type=f32)
            decay_x = jnp.dot(e_chunk, e_ref[...], preferred_element_type=f32)[0:1, :]
            states = []
            for g in range(SSD_GROUPS):
                lo = g * GROUP_W
                bm = x_ref[rr, D_SSD + g * SSD_STATE:D_SSD + (g + 1) * SSD_STATE]
                xw = (x_ref[rr, lo:lo + GROUP_W].astype(f32) * w_x[:, lo:lo + GROUP_W]).astype(bf16)
                states.append(lax.dot_general(bm, xw, (((0,), (0,)), ((), ())), preferred_element_type=f32))
            per_dir.append((decay_x, states))
        steps.append(per_dir)
    for i in range(STATE_CHUNKS):
        for d, h_ref, slot in ((0, hf_ref, i), (1, hb_ref, STATE_CHUNKS - 1 - i)):
            decay_x, states = steps[i][d]
            for g in range(SSD_GROUPS):
                lo = g * GROUP_W
                h_in = state_ref[d, g]
                h_ref[slot, g] = h_in.astype(bf16)
                state_ref[d, g] = h_in * decay_x[:, lo:lo + GROUP_W] + states[g]


MAIN_CHUNKS = 4


def _ssd_chunk_kernel(x_ref, dt_ref, dtt_ref, hf_ref, hb_ref, sz_ref, arow_ref, acol_ref, tri_l3_ref, tri_u3_ref,
                      ef_ref, eb_ref, dskip_ref, g_ref, y_ref, rows_ref, acc_ref):
    for i in range(MAIN_CHUNKS):
        rows = pl.ds(i * CHUNK, CHUNK)
        _ssd_one_chunk(x_ref.at[rows, :], dt_ref.at[rows, :], dtt_ref.at[:, rows], hf_ref.at[i], hb_ref.at[i],
                       sz_ref.at[rows, :], arow_ref, acol_ref, tri_l3_ref, tri_u3_ref, ef_ref, eb_ref,
                       dskip_ref, g_ref, y_ref.at[rows, :], rows_ref.at[i], acc_ref.at[i])


def _ssd_one_chunk(x_ref, dt_ref, dtt_ref, hf_ref, hb_ref, sz_ref, arow_ref, acol_ref, tri_l3_ref, tri_u3_ref,
                   ef_ref, eb_ref, dskip_ref, g_ref, y_ref, rows_ref, acc_ref):
    q = CHUNK
    dt = dt_ref[...]
    dtt = dtt_ref[...]
    c, _ = _decay_compact(dt * arow_ref[...], tri_l3_ref)
    at = dtt * acol_ref[...]
    hi, mid, lo = _split3(at)
    cst = jnp.dot(jnp.concatenate([hi, mid, lo], axis=1), tri_u3_ref[...], preferred_element_type=f32)
    row = lax.broadcasted_iota(jnp.int32, at.shape, 0)
    ct = jnp.where(row >= SSD_HEADS, cst[:, q - 1:q] - cst + at, cst)
    ldt = jnp.log2(dtt)

    lane8 = lax.broadcasted_iota(jnp.int32, (N_PAIRS, LANES), 1)
    kind = 0
    for src in (ct - ldt, ldt):
        for d in range(2):
            even = src[d * SSD_HEADS:d * SSD_HEADS + N_PAIRS, :]
            odd = src[d * SSD_HEADS + N_PAIRS:(d + 1) * SSD_HEADS, :]
            rows_ref[pl.ds(kind * N_PAIRS, N_PAIRS), :] = jnp.where(lane8 < HALF, even, pltpu.roll(odd, HALF, 1))
            rows_ref[pl.ds((kind + 1) * N_PAIRS, N_PAIRS), :] = jnp.where(lane8 < HALF, pltpu.roll(even, HALF, 1), odd)
            kind += 2

    c3 = _pack3(c)
    cx = (jnp.dot(c3, ef_ref[...], preferred_element_type=f32),
          jnp.dot(c3, eb_ref[...], preferred_element_type=f32))

    l_in_half = lax.broadcasted_iota(jnp.int32, (HALF, LANES), 0)
    lane_h = lax.broadcasted_iota(jnp.int32, (HALF, LANES), 1)
    s_in_half = lane_h % HALF
    zero_h = jnp.zeros((HALF, LANES), bf16)
    for g in range(SSD_GROUPS):
        bm = x_ref[:, D_SSD + g * SSD_STATE:D_SSD + (g + 1) * SSD_STATE]
        cm = x_ref[:, D_SSD + BC_DIM + g * SSD_STATE:D_SSD + BC_DIM + (g + 1) * SSD_STATE]
        sc = []
        for half in range(2):
            b_half = bm[half * HALF:(half + 1) * HALF, :]
            b_dup = jnp.concatenate([b_half, b_half], axis=0)
            sc.append(lax.dot_general(cm, b_dup, (((1,), (1,)), ((), ())), preferred_element_type=f32))
        low = [jnp.where(l_in_half >= s_in_half, sc[h][h * HALF:(h + 1) * HALF, :], 0.0) for h in range(2)]
        up = [jnp.where(l_in_half <= s_in_half, sc[h][h * HALF:(h + 1) * HALF, :], 0.0) for h in range(2)]
        lo_g = g * GROUP_W
        y_off = (jnp.dot(cm, hf_ref[g], preferred_element_type=f32) * jnp.exp2(cx[0][:, lo_g:lo_g + GROUP_W])
                 + jnp.dot(cm, hb_ref[g], preferred_element_type=f32) * jnp.exp2(cx[1][:, lo_g:lo_g + GROUP_W]))
        acc_ref[:, lo_g:lo_g + GROUP_W] = y_off
        for jj in range(PAIRS_PER_GROUP):
            j = g * PAIRS_PER_GROUP + jj
            lo_j = j * LANES
            row_vec = lambda k: rows_ref[pl.ds(k * N_PAIRS + j, 1), :]
            quads = [[None, None], [None, None]]
            for th in range(2):
                rows_t = slice(th * HALF, (th + 1) * HALF)
                for sh in range(2):
                    if th == sh:
                        seg_f = jnp.minimum(cx[0][rows_t, lo_j:lo_j + LANES] - row_vec(0 + sh), row_vec(4 + sh))
                        seg_b = jnp.minimum(cx[1][rows_t, lo_j:lo_j + LANES] - row_vec(2 + sh), row_vec(6 + sh))
                        t = low[sh] * jnp.exp2(seg_f) + up[sh] * jnp.exp2(seg_b)
                    else:
                        d = 0 if th > sh else 1
                        seg = cx[d][rows_t, lo_j:lo_j + LANES] - row_vec(2 * d + sh)
                        t = sc[sh][rows_t, :] * jnp.exp2(seg)
                    quads[th][sh] = t.astype(bf16)
            lhs = jnp.concatenate([jnp.concatenate(quads[0], axis=1), jnp.concatenate(quads[1], axis=1)], axis=0)
            xp = x_ref[:, lo_j:lo_j + LANES]
            blocks = []
            for sh in range(2):
                xh = xp[sh * HALF:(sh + 1) * HALF, :]
                blocks.append(jnp.where(lane_h < HALF, xh, zero_h))
                blocks.append(jnp.where(lane_h >= HALF, xh, zero_h))
            rhs = jnp.concatenate(blocks, axis=0)
            acc_ref[:, lo_j:lo_j + LANES] += jnp.dot(lhs, rhs, preferred_element_type=f32)

    y = acc_ref[...] + dskip_ref[...] * x_ref[:, 0:D_SSD].astype(f32)
    y = y * sz_ref[...].astype(f32)
    y = y * lax.rsqrt(jnp.mean(y * y, axis=-1, keepdims=True) + EPS) * g_ref[...]
    y_ref[...] = y.astype(bf16)


def _ssd(xbc_c, dt, dtt, sz, a_row, a_col, consts, dskip_lanes, g_ssd, nb, seq):
    rows = nb * seq
    nc = seq // CHUNK
    state_shape = (SSD_GROUPS, SSD_STATE, GROUP_W)
    assert nc % STATE_CHUNKS == 0
    nblk = nc // STATE_CHUNKS
    blk_rows = STATE_CHUNKS * CHUNK
    x_spec = lambda pos: pl.BlockSpec((blk_rows, XBC_DIM), lambda b, k: (b * nblk + pos(k), 0))
    dt_spec = lambda pos: pl.BlockSpec((blk_rows, LANES), lambda b, k: (b * nblk + pos(k), 0))
    h_spec = lambda pos: pl.BlockSpec((STATE_CHUNKS,) + state_shape, lambda b, k: (b * nblk + pos(k), 0, 0, 0))
    fwd = lambda k: k
    rev = lambda k: nblk - 1 - k
    h_shape = jax.ShapeDtypeStruct((nb * nc,) + state_shape, bf16)
    h_f, h_b = pl.pallas_call(
        _ssd_state_kernel,
        out_shape=(h_shape, h_shape),
        grid=(nb, nblk),
        in_specs=[x_spec(fwd), dt_spec(fwd), x_spec(rev), dt_spec(rev),
                  _const_spec((1, LANES)), _const_spec((CHUNK, 3 * CHUNK)),
                  _const_spec((LANES, D_SSD)), _const_spec((LANES, D_SSD))],
        out_specs=(h_spec(fwd), h_spec(rev)),
        scratch_shapes=[pltpu.VMEM((2,) + state_shape, f32)],
        compiler_params=_params(("parallel", "arbitrary")),
        name="ssd_state_scan",
    )(xbc_c, dt, xbc_c, dt, a_row, consts["tri_l3"], consts["expand_f"], consts["expand_b"])

    assert nc % MAIN_CHUNKS == 0
    step_rows = MAIN_CHUNKS * CHUNK
    row_spec = lambda width: pl.BlockSpec((step_rows, width), lambda i: (i, 0))
    hc_spec = pl.BlockSpec((MAIN_CHUNKS,) + state_shape, lambda i: (i, 0, 0, 0))
    return pl.pallas_call(
        _ssd_chunk_kernel,
        out_shape=jax.ShapeDtypeStruct((rows, D_SSD), bf16),
        grid=(nb * nc // MAIN_CHUNKS,),
        in_specs=[row_spec(XBC_DIM), row_spec(LANES), pl.BlockSpec((2 * SSD_HEADS, step_rows), lambda i: (0, i)),
                  hc_spec, hc_spec, row_spec(D_SSD),
                  _const_spec((1, LANES)), _const_spec((2 * SSD_HEADS, 1)),
                  _const_spec((CHUNK, 3 * CHUNK)), _const_spec((3 * CHUNK, CHUNK)),
                  _const_spec((LANES, D_SSD)), _const_spec((LANES, D_SSD)),
                  _const_spec((1, D_SSD)), _const_spec((1, D_SSD))],
        out_specs=row_spec(D_SSD),
        scratch_shapes=[pltpu.VMEM((MAIN_CHUNKS, 8 * N_PAIRS, LANES), f32),
                        pltpu.VMEM((MAIN_CHUNKS, CHUNK, D_SSD), f32)],
        compiler_params=_params(("parallel",)),
        name="ssd_chunk",
    )(xbc_c, dt, dtt, h_f, h_b, sz, a_row, a_col, consts["tri_l3"], consts["tri_u3"],
      consts["expand_f"], consts["expand_b"], dskip_lanes, g_ssd)


def _seq_dft_kernel(a_ref, r_ref, o_ref, acc_ref, *, scale):
    k = pl.program_id(2)

    @pl.when(k == 0)
    def _():
        acc_ref[...] = jnp.zeros_like(acc_ref)

    acc_ref[...] += jnp.dot(a_ref[...], r_ref[...], preferred_element_type=f32)

    @pl.when(k == pl.num_programs(2) - 1)
    def _():
        o_ref[...] = (acc_ref[...] * scale).astype(bf16)


def _seq_dft(dft_l, r, seq):
    cols = r.shape[1]
    tm = min(1024, seq)
    tn = min(1024, cols)
    tk = min(2048, 2 * seq)
    scale = 1.0 / math.sqrt(seq * FOURIER_GROUP_DIM)
    return pl.pallas_call(
        functools.partial(_seq_dft_kernel, scale=scale),
        out_shape=jax.ShapeDtypeStruct((seq, cols), bf16),
        grid=(seq // tm, cols // tn, 2 * seq // tk),
        in_specs=[pl.BlockSpec((tm, tk), lambda i, j, k: (i, k)), pl.BlockSpec((tk, tn), lambda i, j, k: (k, j))],
        out_specs=pl.BlockSpec((tm, tn), lambda i, j, k: (i, j)),
        scratch_shapes=[pltpu.VMEM((tm, tn), f32)],
        compiler_params=_params(("parallel", "parallel", "arbitrary")),
        name="fourier_seq_dft",
    )(dft_l, r)


def _seq_dft_matrix(seq):
    half = seq // 2 + 1
    j = lax.broadcasted_iota(jnp.int32, (half, seq), 0)
    k = lax.broadcasted_iota(jnp.int32, (half, seq), 1)
    ang = ((j * k) % seq).astype(f32) * (2.0 * math.pi / seq)
    cos_t, sin_t = jnp.cos(ang).astype(bf16), jnp.sin(ang).astype(bf16)
    mirror = lambda t: jnp.flip(t[1:seq // 2], axis=0)
    cos_l = jnp.concatenate([cos_t, mirror(cos_t)], axis=0)
    sin_l = jnp.concatenate([sin_t, -mirror(sin_t)], axis=0)
    return jnp.concatenate([cos_l, -sin_l], axis=1)


FFN_CHUNK = D_FF // 2
POST_TILE = 512


def _post_kernel(ap_ref, a_ref, an_ref, x_ref, mod_ref, yb_ref, fc_ref, gate_ref,
                 cw_ref, cb_ref, lg_ref, lb_ref, wa_ref, wb_ref, wc_ref, wo_ref, gmix_ref,
                 gpre_ref, wi_ref, wf_ref, gffn_ref, o_ref, pad_ref, shift_ref, *, tm, tiles_per_seq):
    i = pl.program_id(0)
    first = (i % tiles_per_seq) == 0
    last = (i % tiles_per_seq) == tiles_per_seq - 1
    a2 = _conv_branch(ap_ref, a_ref, an_ref, cw_ref, cb_ref, lg_ref, lb_ref, pad_ref, shift_ref, tm, first, last)
    o_a = jnp.dot(a2, wa_ref[...], preferred_element_type=f32)
    merged = gate_ref[:, 0:D_MODEL].astype(f32) * o_a
    o_b = jnp.dot(yb_ref[...], wb_ref[...], preferred_element_type=f32)
    merged = merged + gate_ref[:, D_MODEL:2 * D_MODEL].astype(f32) * o_b
    o_c = jnp.dot(fc_ref[...], wc_ref[...], preferred_element_type=f32)
    merged = merged + gate_ref[:, 2 * D_MODEL:3 * D_MODEL].astype(f32) * o_c
    m = jnp.dot(merged.astype(bf16), wo_ref[...], preferred_element_type=f32)
    mn = m * lax.rsqrt(jnp.mean(m * m, axis=-1, keepdims=True) + EPS) * gmix_ref[...]
    x = x_ref[...] + mod_ref[2:3, :] * mn

    hn = x * lax.rsqrt(jnp.mean(x * x, axis=-1, keepdims=True) + EPS) * gpre_ref[...]
    h = (hn * (1.0 + mod_ref[4:5, :]) + mod_ref[3:4, :]).astype(bf16)
    f = None
    for c in range(D_FF // FFN_CHUNK):
        lo = c * FFN_CHUNK
        gate = jnp.dot(h, wi_ref[:, lo:lo + FFN_CHUNK], preferred_element_type=f32)
        up = jnp.dot(h, wi_ref[:, D_FF + lo:D_FF + lo + FFN_CHUNK], preferred_element_type=f32)
        act = (gate * _sigmoid(gate) * up).astype(bf16)
        part = jnp.dot(act, wf_ref[lo:lo + FFN_CHUNK, :], preferred_element_type=f32)
        f = part if f is None else f + part
    fn = f * lax.rsqrt(jnp.mean(f * f, axis=-1, keepdims=True) + EPS) * gffn_ref[...]
    o_ref[...] = x + mod_ref[5:6, :] * fn


def _post(a, x, mod, yb, fc, gates, lw, nb, seq):
    rows = nb * seq
    tm = min(POST_TILE, seq)
    tiles_per_seq = seq // tm
    row_spec = lambda width: pl.BlockSpec((tm, width), lambda i: (i, 0))
    vec = lambda width: _const_spec((1, width))
    return pl.pallas_call(
        functools.partial(_post_kernel, tm=tm, tiles_per_seq=tiles_per_seq),
        out_shape=jax.ShapeDtypeStruct((rows, D_MODEL), f32),
        grid=(rows // tm,),
        in_specs=_halo_specs(tm, CONV_A_HALO, D_A, rows) + [
            row_spec(D_MODEL),
            pl.BlockSpec((None, 6, D_MODEL), lambda i: (i // tiles_per_seq, 0, 0)),
            row_spec(D_SSD),
            pl.BlockSpec((tm, D_C), lambda i: (i % tiles_per_seq, i // tiles_per_seq)),
            row_spec(3 * D_MODEL),
            _const_spec((CONV_A_WIDTH, D_A)), vec(D_A), vec(D_A), vec(D_A),
            _const_spec((D_A, D_MODEL)), _const_spec((D_SSD, D_MODEL)), _const_spec((D_C, D_MODEL)),
            _const_spec((D_MODEL, D_MODEL)), vec(D_MODEL),
            vec(D_MODEL), _const_spec((D_MODEL, 2 * D_FF)), _const_spec((D_FF, D_MODEL)), vec(D_MODEL),
        ],
        out_specs=row_spec(D_MODEL),
        scratch_shapes=[pltpu.VMEM((tm + 2 * CONV_A_HALO, D_A), f32),
                        pltpu.VMEM((SUBLANES - 1, tm + 2 * CONV_A_HALO, D_A), f32)],
        compiler_params=_params(("parallel",)),
        name="conv_merge_ffn",
    )(a, a, a, x, mod, yb, fc, gates,
      lw["conv_a_w"], lw["conv_a_b"], lw["ln_a_g"], lw["ln_a_b"],
      lw["w_a_out"], lw["w_b_out"], lw["w_c_out"], lw["w_out"], lw["g_post_mix"],
      lw["g_pre_ffn"], lw["w_ffn_in"], lw["w_ffn_out"], lw["g_post_ffn"])


def _dft_cos_sin(n):
    j = lax.broadcasted_iota(jnp.int32, (n, n), 0)
    k = lax.broadcasted_iota(jnp.int32, (n, n), 1)
    ang = ((j * k) % n).astype(f32) * (2.0 * math.pi / n)
    return jnp.cos(ang), jnp.sin(ang)


def _layer(x, mod, lw, consts, nb, seq):
    a, sz, xbc_c, r, gates, dt, dtt = _in_proj(
        x, mod, lw["g_pre_mix"], lw["w_r"], lw["w_dt"], lw["w_dtt"], lw["dtb"], lw["dtbt"], consts["dft_c"],
        lw["conv_s_w"], lw["conv_s_b"], nb, seq)
    yb = _ssd(xbc_c, dt, dtt, sz, lw["a_row"], lw["a_col"], consts, lw["dskip"], lw["g_ssd"], nb, seq)
    fc = _seq_dft(consts["dft_l"][seq], r.reshape(2 * seq, nb * D_C), seq)
    return _post(a, x, mod, yb, fc, gates, lw, nb, seq)


def kernel(x_prompt, x_sample, c_prompt, c_sample, w_ada, b_ada, g_pre_mix, g_post_mix, g_pre_ffn, g_post_ffn, w_in, conv_a_w, conv_a_b, ln_a_g, ln_a_b, w_a_out, conv_s_w, conv_s_b, dt_bias_f, dt_bias_b, a_log_f, a_log_b, d_skip, g_ssd, w_b_out, w_c_out, w_out, w_ffn_in, w_ffn_out):
    depth = w_in.shape[0]
    assert depth == DEPTH
    bp, lp, _ = x_prompt.shape
    bs, ls, _ = x_sample.shape

    n_seq = bp + bs
    pad_rows = -n_seq % SUBLANES
    c_all = jnp.concatenate([c_prompt, c_sample, jnp.zeros((pad_rows, D_MODEL), f32)], axis=0)
    mod_all = _modulation(c_all, w_ada, b_ada).reshape(depth, n_seq + pad_rows, 6, D_MODEL)

    w_r = jnp.concatenate([w_in[:, :, _OFF_A:_OFF_DT], w_in[:, :, _OFF_UC:_OFF_END]], axis=-1).astype(bf16)
    reps = LANES // REPLICA
    t_order = jnp.array([d * SSD_HEADS + 2 * j + par for d in range(2) for par in range(2) for j in range(N_PAIRS)])
    w_dt_raw = w_in[:, :, _OFF_DT:_OFF_UC]
    w_dt = jnp.tile(w_dt_raw, (1, 1, reps)).astype(bf16)
    w_dtt = jnp.swapaxes(w_dt_raw[:, :, t_order], 1, 2).astype(bf16)
    dtb_cat = jnp.concatenate([dt_bias_f, dt_bias_b], axis=-1)
    dtb = jnp.tile(dtb_cat, (1, reps))
    a_cat = jnp.concatenate([-jnp.exp(a_log_f), -jnp.exp(a_log_b)], axis=-1) * math.log2(math.e)
    a_row = jnp.tile(a_cat, (1, reps))

    cos_c, sin_c = _dft_cos_sin(FOURIER_GROUP_DIM)
    t_i = lax.broadcasted_iota(jnp.int32, (CHUNK, CHUNK), 0)
    u_i = lax.broadcasted_iota(jnp.int32, (CHUNK, CHUNK), 1)
    tri = (u_i <= t_i).astype(bf16)
    src_lane = lax.broadcasted_iota(jnp.int32, (LANES, D_SSD), 0)
    dst_head = lax.broadcasted_iota(jnp.int32, (LANES, D_SSD), 1) // SSD_HEAD_DIM
    live = src_lane < 3 * REPLICA
    consts = {
        "dft_c": jnp.concatenate([cos_c, sin_c], axis=1).astype(bf16),
        "tri_l3": jnp.tile(tri, (1, 3)),
        "tri_u3": jnp.tile(tri.T, (3, 1)),
        "expand_f": (live & (src_lane % REPLICA == dst_head)).astype(bf16),
        "expand_b": (live & (src_lane % REPLICA == dst_head + SSD_HEADS)).astype(bf16),
        "dft_l": {},
    }
    for seq in sorted({lp, ls}):
        consts["dft_l"][seq] = _seq_dft_matrix(seq)

    row = lambda v: v.reshape(1, -1)
    y_prompt = x_prompt.reshape(bp * lp, D_MODEL)
    y_sample = x_sample.reshape(bs * ls, D_MODEL)
    for i in range(depth):
        lw = {
            "g_pre_mix": row(g_pre_mix[i]), "g_post_mix": row(g_post_mix[i]),
            "g_pre_ffn": row(g_pre_ffn[i]), "g_post_ffn": row(g_post_ffn[i]),
            "w_r": w_r[i], "w_dt": w_dt[i], "w_dtt": w_dtt[i],
            "dtb": row(dtb[i]), "dtbt": dtb_cat[i][t_order].reshape(-1, 1),
            "conv_a_w": conv_a_w[i], "conv_a_b": row(conv_a_b[i]), "ln_a_g": row(ln_a_g[i]), "ln_a_b": row(ln_a_b[i]),
            "conv_s_w": conv_s_w[i], "conv_s_b": row(conv_s_b[i]),
            "a_row": row(a_row[i]), "a_col": a_cat[i][t_order].reshape(-1, 1),
            "dskip": row(jnp.repeat(d_skip[i], SSD_HEAD_DIM)), "g_ssd": row(g_ssd[i]),
            "w_a_out": w_a_out[i].astype(bf16), "w_b_out": w_b_out[i].astype(bf16),
            "w_c_out": w_c_out[i].astype(bf16), "w_out": w_out[i].astype(bf16),
            "w_ffn_in": w_ffn_in[i].astype(bf16), "w_ffn_out": w_ffn_out[i].astype(bf16),
        }
        y_prompt = _layer(y_prompt, mod_all[i, :bp], lw, consts, bp, lp)
        y_sample = _layer(y_sample, mod_all[i, bp:bp + bs], lw, consts, bs, ls)
    return (y_prompt.reshape(bp, lp, D_MODEL), y_sample.reshape(bs, ls, D_MODEL))
```
